```python
import math
import jax, jax.numpy as jnp
from jax import lax
import numpy as np

D_MODEL = 2048
BATCH = 2
SEQ = 16384
DEPTH = 1

ATT_Q_HEADS = 16
ATT_KV_HEADS = 2
ATT_HEAD_DIM = 64
WINDOW = 128
ATT_BLOCK = 128
ATT_Q_W = ATT_Q_HEADS * ATT_HEAD_DIM
ATT_KV_W = ATT_KV_HEADS * ATT_HEAD_DIM
ATT_QKV_W = ATT_Q_W + 2 * ATT_KV_W
REL_BUCKETS = 32
REL_MAX_DIST = 128
HGRN_HEADS = 8
HGRN_DK = 128
HGRN_DV = 128
HGRN_W = HGRN_HEADS * HGRN_DV
HGRN_CHUNK = 64
IN_COLS = ATT_QKV_W + 4 * HGRN_W + 2 * D_MODEL
MOE_GROUPS = 8
MOE_EXPERTS_PER_GROUP = 8
MOE_EXPERTS = MOE_GROUPS * MOE_EXPERTS_PER_GROUP
MOE_TOP_K = 2
MOE_HIDDEN = 512
MOE_BLOCK = 128
PLE_DIM = 256
EPS = 1e-6

kernel_name = 'hybrid_swa_hgrn2_hmoe_block'


def rms_norm(x, gain=None):
    xf = x.astype(jnp.float32)
    y = xf * lax.rsqrt(jnp.mean(xf * xf, axis=-1, keepdims=True) + EPS)
    if gain is not None:
        y = y * gain.astype(jnp.float32)
    return y.astype(x.dtype)


def t5_bucket(dist):
    exact = REL_BUCKETS // 2
    d = jnp.maximum(dist, 0)
    large = exact + (jnp.log(jnp.maximum(d, 1).astype(jnp.float32) / exact)
                     / math.log(REL_MAX_DIST / exact) * (REL_BUCKETS - exact)).astype(jnp.int32)
    large = jnp.minimum(large, REL_BUCKETS - 1)
    return jnp.where(d < exact, d, large)


def sliding_window_attention(q, k, v, sinks, rel_table):
    B, T = q.shape[0], q.shape[1]
    nb = T // ATT_BLOCK
    G = ATT_Q_HEADS // ATT_KV_HEADS
    f32 = jnp.float32
    qb = q.astype(f32).reshape(B, nb, ATT_BLOCK, ATT_KV_HEADS, G, ATT_HEAD_DIM) * (ATT_HEAD_DIM ** -0.5)

    def band(a):
        a = a.astype(f32).reshape(B, nb, ATT_BLOCK, ATT_KV_HEADS, ATT_HEAD_DIM)
        prev = jnp.pad(a[:, :-1], ((0, 0), (1, 0), (0, 0), (0, 0), (0, 0)))
        return jnp.concatenate([prev, a], axis=2)

    kb, vb = band(k), band(v)
    s = jnp.einsum('bnqkgd,bnskd->bnkgqs', qb, kb)
    qi = jnp.arange(ATT_BLOCK)[:, None]
    kj = jnp.arange(2 * ATT_BLOCK)[None, :]
    dist = qi + ATT_BLOCK - kj
    in_window = (dist >= 0) & (dist < WINDOW)
    bias = rel_table.astype(f32)[t5_bucket(dist)]
    bias = jnp.transpose(bias, (2, 0, 1)).reshape(ATT_KV_HEADS, G, ATT_BLOCK, 2 * ATT_BLOCK)
    not_pad = (jnp.arange(nb) > 0)[:, None, None] | (kj >= ATT_BLOCK)[None]
    valid = in_window[None] & not_pad
    s = jnp.where(valid[None, :, None, None], s + bias, -jnp.inf)
    sink = sinks.astype(f32).reshape(ATT_KV_HEADS, G)[..., None, None]
    m = jnp.maximum(jnp.max(s, axis=-1, keepdims=True), sink)
    pr = jnp.exp(s - m)
    denom = jnp.sum(pr, axis=-1, keepdims=True) + jnp.exp(sink - m)
    o = jnp.einsum('bnkgqs,bnskd->bnqkgd', pr / denom, vb)
    return o.reshape(B, T, ATT_Q_W).astype(q.dtype)


def hgrn2(q, f_raw, i, lower_bound):
    B, T = q.shape[0], q.shape[1]
    f32 = jnp.float32
    lb = lower_bound.astype(f32).reshape(HGRN_HEADS, HGRN_DK)
    f = lb + (1.0 - lb) * jax.nn.sigmoid(f_raw.astype(f32))
    k = 1.0 - f
    g = jnp.log(f)
    qs = jax.nn.silu(q.astype(f32))
    nc = T // HGRN_CHUNK

    def chunks(a):
        return a.reshape(B, nc, HGRN_CHUNK, HGRN_HEADS, a.shape[-1]).transpose(1, 0, 3, 2, 4)

    xs = (chunks(qs), chunks(k), chunks(i.astype(f32)), chunks(g))
    causal = jnp.tril(jnp.ones((HGRN_CHUNK, HGRN_CHUNK), dtype=bool))
    ref = HGRN_CHUNK // 2 - 1

    def step(S, inp):
        qc, kc, vc, gc = inp
        b = jnp.cumsum(gc, axis=-2)
        b_ref = b[..., ref:ref + 1, :]
        b_last = b[..., -1:, :]
        o_inter = jnp.einsum('bhtk,bhkv->bhtv', qc * jnp.exp(b), S)
        a = jnp.einsum('bhtk,bhsk->bhts', qc * jnp.exp(b - b_ref), kc * jnp.exp(b_ref - b))
        o_intra = jnp.einsum('bhts,bhsv->bhtv', jnp.where(causal, a, 0.0), vc)
        S = jnp.exp(b_last)[:, :, 0, :, None] * S + jnp.einsum('bhsk,bhsv->bhkv', kc * jnp.exp(b_last - b), vc)
        return S, o_inter + o_intra

    S0 = jnp.zeros((B, HGRN_HEADS, HGRN_DK, HGRN_DV), f32)
    _, o = lax.scan(step, S0, xs)
    return o.transpose(1, 0, 3, 2, 4).reshape(B, T, HGRN_HEADS, HGRN_DV)


def hier_moe(xn, w_coarse, b_coarse, w_fine, b_fine, w_gate, w_up, w_down):
    B, T, D = xn.shape
    n_tok = B * T
    f32 = jnp.float32
    xt = xn.reshape(n_tok, D)
    p_group = jax.nn.softmax((xt @ w_coarse + b_coarse).astype(f32), axis=-1)
    grp = jnp.argmax(p_group, axis=-1).astype(jnp.int32)
    p_sel = jnp.max(p_group, axis=-1)
    fine = (xt @ w_fine + b_fine).astype(f32).reshape(n_tok, MOE_GROUPS, MOE_EXPERTS_PER_GROUP)
    fine = jnp.take_along_axis(fine, grp[:, None, None], axis=1)[:, 0]
    top_v, top_i = lax.top_k(fine, MOE_TOP_K)
    wts = p_sel[:, None] * jax.nn.softmax(top_v, axis=-1)
    expert = grp[:, None] * MOE_EXPERTS_PER_GROUP + top_i.astype(jnp.int32)
    n_asg = n_tok * MOE_TOP_K
    flat_e = expert.reshape(-1)
    flat_t = jnp.repeat(jnp.arange(n_tok, dtype=jnp.int32), MOE_TOP_K)
    flat_w = wts.reshape(-1)
    order = jnp.argsort(flat_e)
    se = flat_e[order]
    counts = jnp.bincount(flat_e, length=MOE_EXPERTS)
    padded = (counts + MOE_BLOCK - 1) // MOE_BLOCK * MOE_BLOCK
    start = jnp.cumsum(counts) - counts
    pad_end = jnp.cumsum(padded)
    pad_start = pad_end - padded
    dest = pad_start[se] + jnp.arange(n_asg, dtype=jnp.int32) - start[se]
    cap = n_asg + MOE_EXPERTS * MOE_BLOCK
    nblk = cap // MOE_BLOCK
    buf_t = jnp.zeros((cap,), jnp.int32).at[dest].set(flat_t[order])
    buf_w = jnp.zeros((cap,), f32).at[dest].set(flat_w[order])
    blk_e = jnp.minimum(jnp.searchsorted(pad_end, jnp.arange(nblk) * MOE_BLOCK, side='right'),
                        MOE_EXPERTS - 1).astype(jnp.int32)

    def run_block(args):
        tok, e, wt = args
        xb = xt[tok]
        hid = jax.nn.silu(xb @ w_gate[e]) * (xb @ w_up[e])
        return ((hid @ w_down[e]) * wt[:, None]).astype(xt.dtype)

    y = lax.map(run_block, (buf_t.reshape(nblk, MOE_BLOCK), blk_e, buf_w.reshape(nblk, MOE_BLOCK)))
    out = jnp.zeros_like(xt).at[buf_t].add(y.reshape(cap, D))
    return out.reshape(B, T, D)


def setup_inputs(seed: int = 0) -> dict:
    key = jax.random.key(seed)
    ks = jax.random.split(key, 24)
    f32 = jnp.float32
    nrm = lambda k, shape, s: (jax.random.normal(k, shape, f32) * s).astype(f32)
    gain = lambda k, shape: 1.0 + 0.05 * jax.random.normal(k, shape, f32)
    D = D_MODEL
    return {
        'x': nrm(ks[0], (BATCH, SEQ, D), 1.0),
        'p': nrm(ks[1], (DEPTH, BATCH, SEQ, PLE_DIM), 1.0),
        'g_mix': gain(ks[2], (DEPTH, D)),
        'w_in': nrm(ks[3], (DEPTH, D, IN_COLS), D ** -0.5),
        'b_qkv': nrm(ks[4], (DEPTH, ATT_QKV_W), 0.02),
        'sinks': nrm(ks[5], (DEPTH, ATT_Q_HEADS), 1.0),
        'rel_bias': nrm(ks[6], (REL_BUCKETS, ATT_Q_HEADS), 0.5),
        'lb_logits': nrm(ks[7], (DEPTH + 1, HGRN_HEADS * HGRN_DK), 0.5),
        'g_hgrn': gain(ks[8], (DEPTH, HGRN_DV)),
        'w_up_att': nrm(ks[9], (DEPTH, ATT_Q_W, D), ATT_Q_W ** -0.5),
        'w_up_rec': nrm(ks[10], (DEPTH, HGRN_W, D), HGRN_W ** -0.5),
        'w_out': nrm(ks[11], (DEPTH, D, D), D ** -0.5),
        'g_ffn': gain(ks[12], (DEPTH, D)),
        'w_coarse': nrm(ks[13], (DEPTH, D, MOE_GROUPS), D ** -0.5),
        'b_coarse': nrm(ks[14], (DEPTH, MOE_GROUPS), 0.01),
        'w_fine': nrm(ks[15], (DEPTH, D, MOE_EXPERTS), D ** -0.5),
        'b_fine': nrm(ks[16], (DEPTH, MOE_EXPERTS), 0.01),
        'w_gate': nrm(ks[17], (DEPTH, MOE_EXPERTS, D, MOE_HIDDEN), D ** -0.5),
        'w_up': nrm(ks[18], (DEPTH, MOE_EXPERTS, D, MOE_HIDDEN), D ** -0.5),
        'w_down': nrm(ks[19], (DEPTH, MOE_EXPERTS, MOE_HIDDEN, D), MOE_HIDDEN ** -0.5),
        'w_ple_in': nrm(ks[20], (DEPTH, PLE_DIM, D), PLE_DIM ** -0.5),
        'g_ple': gain(ks[21], (DEPTH, D)),
        'w_ple_gate': nrm(ks[22], (DEPTH, D, D), D ** -0.5),
        'g_final': gain(ks[23], (D,)),
    }


def reference(x, p, g_mix, w_in, b_qkv, sinks, rel_bias, lb_logits, g_hgrn, w_up_att, w_up_rec,
              w_out, g_ffn, w_coarse, b_coarse, w_fine, b_fine, w_gate, w_up, w_down,
              w_ple_in, g_ple, w_ple_gate, g_final):
    B, T = x.shape[0], x.shape[1]
    lower_bounds = jnp.cumsum(jax.nn.softmax(lb_logits.astype(jnp.float32), axis=0), axis=0)
    h = x
    for layer in range(DEPTH):
        u = rms_norm(h, g_mix[layer])
        z = u @ w_in[layer]
        qkv = z[..., :ATT_QKV_W] + b_qkv[layer]
        qa = qkv[..., :ATT_Q_W].reshape(B, T, ATT_Q_HEADS, ATT_HEAD_DIM)
        ka = qkv[..., ATT_Q_W:ATT_Q_W + ATT_KV_W].reshape(B, T, ATT_KV_HEADS, ATT_HEAD_DIM)
        va = qkv[..., ATT_Q_W + ATT_KV_W:].reshape(B, T, ATT_KV_HEADS, ATT_HEAD_DIM)
        off = ATT_QKV_W
        qr, fr, ir, gr = [z[..., off + j * HGRN_W: off + (j + 1) * HGRN_W].reshape(B, T, HGRN_HEADS, HGRN_DV)
                          for j in range(4)]
        off2 = off + 4 * HGRN_W
        gate_att = jax.nn.sigmoid(z[..., off2:off2 + D_MODEL])
        gate_rec = jax.nn.sigmoid(z[..., off2 + D_MODEL:])
        y_att = sliding_window_attention(qa, ka, va, sinks[layer], rel_bias) @ w_up_att[layer]
        o_rec = hgrn2(qr, fr, ir, lower_bounds[layer])
        o_rec = rms_norm(o_rec, g_hgrn[layer]) * jax.nn.silu(gr.astype(jnp.float32))
        y_rec = o_rec.reshape(B, T, HGRN_W).astype(h.dtype) @ w_up_rec[layer]
        h = h + (gate_att * y_att + gate_rec * y_rec) @ w_out[layer]
        h = h + hier_moe(rms_norm(h, g_ffn[layer]), w_coarse[layer], b_coarse[layer], w_fine[layer],
                         b_fine[layer], w_gate[layer], w_up[layer], w_down[layer])
        ple = rms_norm(p[layer] @ w_ple_in[layer], g_ple[layer])
        h = h + ple * jax.nn.sigmoid(rms_norm(h) @ w_ple_gate[layer])
    return rms_norm(h, g_final)
```

```python
import functools
import math

import numpy as np
import jax
import jax.numpy as jnp
from jax import lax
from jax.experimental import pallas as pl
from jax.experimental.pallas import tpu as pltpu

F32 = jnp.float32
BF16 = jnp.bfloat16
I32 = jnp.int32
U32 = jnp.uint32

D_MODEL = 2048
ATT_Q_HEADS = 16
ATT_KV_HEADS = 2
ATT_GROUP = ATT_Q_HEADS // ATT_KV_HEADS
ATT_HEAD_DIM = 64
WINDOW = 128
ATT_BLOCK = 128
ATT_Q_W = ATT_Q_HEADS * ATT_HEAD_DIM
ATT_KV_W = ATT_KV_HEADS * ATT_HEAD_DIM
ATT_QKV_W = ATT_Q_W + 2 * ATT_KV_W
REL_BUCKETS = 32
REL_MAX_DIST = 128
HGRN_HEADS = 8
HGRN_DK = 128
HGRN_DV = 128
HGRN_W = HGRN_HEADS * HGRN_DV
HGRN_CHUNK = 64
MOE_GROUPS = 8
MOE_EXPERTS_PER_GROUP = 8
MOE_EXPERTS = MOE_GROUPS * MOE_EXPERTS_PER_GROUP
MOE_HIDDEN = 512
PLE_DIM = 256
EPS = 1e-6

HALF = D_MODEL // 2
ROUTER_W = 128
EXPERT_ROWS = 256
VMEM_LIMIT = 56 * 1024 * 1024


def _cparams(*sem):
    return pltpu.CompilerParams(dimension_semantics=sem, vmem_limit_bytes=VMEM_LIMIT)


def _rms(xf):
    return xf * lax.rsqrt(jnp.mean(xf * xf, axis=-1, keepdims=True) + EPS)


def _sigmoid(x):
    return 1.0 / (1.0 + jnp.exp(-x))


def _silu(x):
    return x * _sigmoid(x)


def _pack_bf16_pair(lo_f32, hi_f32):
    lo = lax.bitcast_convert_type(lo_f32.astype(BF16).astype(F32), U32)
    hi = lax.bitcast_convert_type(hi_f32.astype(BF16).astype(F32), U32)
    return hi | (lo >> 16)


def _unpack_bf16_pair(w):
    lo = lax.bitcast_convert_type(w << 16, F32)
    hi = lax.bitcast_convert_type(w & jnp.uint32(0xFFFF0000), F32)
    return lo, hi


def _qkv_kernel(x_ref, g_ref, w_ref, b_ref, o_ref):
    u = (_rms(x_ref[...]) * g_ref[...]).astype(BF16)
    acc = jnp.dot(u, w_ref[...], preferred_element_type=F32)
    o_ref[...] = (acc + b_ref[...]).astype(o_ref.dtype)


def _qkv_proj(x2, g_mix, w_qkv, b_qkv, tm):
    n = x2.shape[0]
    return pl.pallas_call(
        _qkv_kernel,
        grid=(n // tm,),
        in_specs=[
            pl.BlockSpec((tm, D_MODEL), lambda i: (i, 0)),
            pl.BlockSpec((1, D_MODEL), lambda i: (0, 0)),
            pl.BlockSpec((D_MODEL, ATT_QKV_W), lambda i: (0, 0)),
            pl.BlockSpec((1, ATT_QKV_W), lambda i: (0, 0)),
        ],
        out_specs=pl.BlockSpec((tm, ATT_QKV_W), lambda i: (i, 0)),
        out_shape=jax.ShapeDtypeStruct((n, ATT_QKV_W), BF16),
        compiler_params=_cparams("parallel"),
        name="qkv_proj",
    )(x2, g_mix, w_qkv, b_qkv)


HZ_TN = 1024
HZ_COLS = 4 * HGRN_W + 2 * D_MODEL
HZ_F_BLOCK = 1


def _hz_kernel(x_ref, g_ref, w_ref, hz_ref, f_ref, u_scr):
    j = pl.program_id(1)

    @pl.when(j == 0)
    def _():
        u_scr[...] = (_rms(x_ref[...]) * g_ref[...]).astype(BF16)

    acc = jnp.dot(u_scr[...], w_ref[...], preferred_element_type=F32)
    hz_ref[...] = acc.astype(BF16)

    @pl.when(j == HZ_F_BLOCK)
    def _():
        f_ref[...] = acc


def _hz_proj(x2, g_mix, w_hz, tm):
    n = x2.shape[0]
    return pl.pallas_call(
        _hz_kernel,
        grid=(n // tm, HZ_COLS // HZ_TN),
        in_specs=[
            pl.BlockSpec((tm, D_MODEL), lambda i, j: (i, 0)),
            pl.BlockSpec((1, D_MODEL), lambda i, j: (0, 0)),
            pl.BlockSpec((D_MODEL, HZ_TN), lambda i, j: (0, j)),
        ],
        out_specs=[
            pl.BlockSpec((tm, HZ_TN), lambda i, j: (i, j)),
            pl.BlockSpec((tm, HGRN_W), lambda i, j: (i, 0)),
        ],
        out_shape=[
            jax.ShapeDtypeStruct((n, HZ_COLS), BF16),
            jax.ShapeDtypeStruct((n, HGRN_W), F32),
        ],
        scratch_shapes=[pltpu.VMEM((tm, D_MODEL), BF16)],
        compiler_params=_cparams("parallel", "arbitrary"),
        name="hz_proj",
    )(x2, g_mix, w_hz)


def _t5_bucket_table():
    qi = np.arange(ATT_BLOCK)[:, None]
    kj = np.arange(2 * ATT_BLOCK)[None, :]
    dist = qi + ATT_BLOCK - kj
    exact = REL_BUCKETS // 2
    d = np.maximum(dist, 0)
    large = exact + (np.log(np.maximum(d, 1).astype(np.float32) / exact)
                     / math.log(REL_MAX_DIST / exact) * (REL_BUCKETS - exact)).astype(np.int32)
    large = np.minimum(large, REL_BUCKETS - 1)
    return np.where(d < exact, d, large).astype(np.int32)


def _bias_kernel(bucket_ref, rel_ref, o_ref):
    bk = bucket_ref[...]
    qi = lax.broadcasted_iota(I32, bk.shape, 0)
    kj = lax.broadcasted_iota(I32, bk.shape, 1)
    dist = qi + ATT_BLOCK - kj
    in_window = (dist >= 0) & (dist < WINDOW)
    first_block = in_window & (kj >= ATT_BLOCK)
    neg = jnp.float32(-jnp.inf)
    for h in range(ATT_Q_HEADS):
        acc = jnp.zeros(bk.shape, F32)
        for b in range(REL_BUCKETS):
            acc = jnp.where(bk == b, rel_ref[b, h], acc)
        o_ref[0, h] = jnp.where(first_block, acc, neg)
        o_ref[1, h] = jnp.where(in_window, acc, neg)


def _bias_table(rel_bias):
    bucket = jnp.asarray(_t5_bucket_table())
    return pl.pallas_call(
        _bias_kernel,
        in_specs=[
            pl.BlockSpec(memory_space=pltpu.VMEM),
            pl.BlockSpec(memory_space=pltpu.SMEM),
        ],
        out_specs=pl.BlockSpec(memory_space=pltpu.VMEM),
        out_shape=jax.ShapeDtypeStruct((2, ATT_Q_HEADS, ATT_BLOCK, 2 * ATT_BLOCK), F32),
        name="bias_table",
    )(bucket, rel_bias)


def _swa_kernel(q_ref, kvc_ref, kvp_ref, bias_ref, sink_ref, o_ref):
    hd = ATT_HEAD_DIM
    scale = hd ** -0.5
    for kk in range(ATT_KV_HEADS):
        k = jnp.concatenate([kvp_ref[:, hd * kk:hd * (kk + 1)], kvc_ref[:, hd * kk:hd * (kk + 1)]], axis=0)
        v = jnp.concatenate([kvp_ref[:, ATT_KV_W + hd * kk:ATT_KV_W + hd * (kk + 1)],
                             kvc_ref[:, ATT_KV_W + hd * kk:ATT_KV_W + hd * (kk + 1)]], axis=0)
        for g in range(ATT_GROUP):
            h = kk * ATT_GROUP + g
            q = q_ref[:, hd * h:hd * (h + 1)]
            s = lax.dot_general(q, k, (((1,), (1,)), ((), ())), preferred_element_type=F32)
            s = s * scale + bias_ref[0, h]
            sink = sink_ref[0, h]
            m = jnp.maximum(jnp.max(s, axis=-1, keepdims=True), sink)
            p = jnp.exp(s - m)
            denom = jnp.sum(p, axis=-1, keepdims=True) + jnp.exp(sink - m)
            o = jnp.dot(p.astype(BF16), v, preferred_element_type=F32) / denom
            o_ref[:, hd * h:hd * (h + 1)] = o.astype(o_ref.dtype)


def _swa(qkv, bias_tab, sinks, batch, seq):
    n = qkv.shape[0]
    nb = seq // ATT_BLOCK
    kv_col = ATT_Q_W // (2 * ATT_KV_W)

    return pl.pallas_call(
        _swa_kernel,
        grid=(batch, nb),
        in_specs=[
            pl.BlockSpec((ATT_BLOCK, ATT_Q_W), lambda b, i: (b * nb + i, 0)),
            pl.BlockSpec((ATT_BLOCK, 2 * ATT_KV_W), lambda b, i: (b * nb + i, kv_col)),
            pl.BlockSpec((ATT_BLOCK, 2 * ATT_KV_W), lambda b, i: (b * nb + jnp.maximum(i - 1, 0), kv_col)),
            pl.BlockSpec((1, ATT_Q_HEADS, ATT_BLOCK, 2 * ATT_BLOCK), lambda b, i: (jnp.minimum(i, 1), 0, 0, 0)),
            pl.BlockSpec(memory_space=pltpu.SMEM),
        ],
        out_specs=pl.BlockSpec((ATT_BLOCK, ATT_Q_W), lambda b, i: (b * nb + i, 0)),
        out_shape=jax.ShapeDtypeStruct((n, ATT_Q_W), BF16),
        compiler_params=_cparams("parallel", "arbitrary"),
        name="swa",
    )(qkv, qkv, qkv, bias_tab, sinks)


def _cumsum_rows(tri_bf16, g):
    g1 = g.astype(BF16)
    r1 = g - g1.astype(F32)
    g2 = r1.astype(BF16)
    r2 = r1 - g2.astype(F32)
    g3 = r2.astype(BF16)
    dot = functools.partial(jnp.dot, preferred_element_type=F32)
    return dot(tri_bf16, g1) + dot(tri_bf16, g2) + dot(tri_bf16, g3)


def _hgrn_kernel(q_ref, f_ref, i_ref, g_ref, lbl_ref, gh_ref, o_ref, st_ref, *, chunks):
    C = HGRN_CHUNK
    ref_row = C // 2 - 1

    @pl.when(pl.program_id(1) == 0)
    def _():
        st_ref[...] = jnp.zeros_like(st_ref)

    l = lbl_ref[...]
    e = jnp.exp(l - jnp.max(l, axis=0, keepdims=True))
    lb_all = e[0:1, :] / jnp.sum(e, axis=0, keepdims=True)

    row = lax.broadcasted_iota(I32, (C, C), 0)
    col = lax.broadcasted_iota(I32, (C, C), 1)
    causal = row >= col
    tri = causal.astype(F32).astype(BF16)
    gain = gh_ref[...]

    def chunk(c, carry):
        r0 = pl.multiple_of(c * C, C)
        rows = pl.ds(r0, C)
        for h in range(HGRN_HEADS):
            cs = slice(HGRN_DK * h, HGRN_DK * (h + 1))
            lb = lb_all[:, cs]
            f = lb + (1.0 - lb) * _sigmoid(f_ref[rows, cs])
            k = 1.0 - f
            b = _cumsum_rows(tri, jnp.log(f))
            qs = _silu(q_ref[rows, cs].astype(F32))
            v = i_ref[rows, cs]
            b_ref = b[ref_row:ref_row + 1, :]
            b_last = b[C - 1:C, :]
            qe = (qs * jnp.exp(b)).astype(BF16)
            qa = (qs * jnp.exp(b - b_ref)).astype(BF16)
            ka = (k * jnp.exp(b_ref - b)).astype(BF16)
            kl = (k * jnp.exp(b_last - b)).astype(BF16)
            st = st_ref[h]
            o_inter = lax.dot_general(qe, st.astype(BF16), (((1,), (1,)), ((), ())),
                                      preferred_element_type=F32)
            a = lax.dot_general(qa, ka, (((1,), (1,)), ((), ())), preferred_element_type=F32)
            a = jnp.where(causal, a, 0.0).astype(BF16)
            o = o_inter + jnp.dot(a, v, preferred_element_type=F32)
            upd = lax.dot_general(v, kl, (((0,), (0,)), ((), ())), preferred_element_type=F32)
            st_ref[h] = st * jnp.exp(b_last) + upd
            on = _rms(o) * gain
            o_ref[rows, cs] = (on * _silu(g_ref[rows, cs].astype(F32))).astype(o_ref.dtype)
        return carry

    lax.fori_loop(0, chunks, chunk, 0)


def _hgrn(hz, f_raw, lb_logits, g_hgrn, batch, seq, tc):
    n = hz.shape[0]
    nt = seq // tc
    blk = lambda col: pl.BlockSpec((tc, HGRN_W), lambda b, t, col=col: (b * nt + t, col))
    return pl.pallas_call(
        functools.partial(_hgrn_kernel, chunks=tc // HGRN_CHUNK),
        grid=(batch, nt),
        in_specs=[
            blk(0),
            pl.BlockSpec((tc, HGRN_W), lambda b, t: (b * nt + t, 0)),
            blk(2),
            blk(3),
            pl.BlockSpec((2, HGRN_W), lambda b, t: (0, 0)),
            pl.BlockSpec((1, HGRN_DV), lambda b, t: (0, 0)),
        ],
        out_specs=pl.BlockSpec((tc, HGRN_W), lambda b, t: (b * nt + t, 0)),
        out_shape=jax.ShapeDtypeStruct((n, HGRN_W), BF16),
        scratch_shapes=[pltpu.VMEM((HGRN_HEADS, HGRN_DV, HGRN_DK), F32)],
        compiler_params=_cparams("parallel", "arbitrary"),
        name="hgrn",
    )(hz, f_raw, hz, hz, lb_logits, g_hgrn)


def _split_hi_lo(a):
    hi = a.astype(BF16)
    lo = (a - hi.astype(F32)).astype(BF16)
    return hi, lo


def _mix_kernel(oa_ref, or_ref, ga_ref, gr_ref, x_ref, wa_ref, wr_ref, wo_ref, gf_ref, wrt_ref, brt_ref,
                h_ref, xn_ref, lg_ref):
    dot = functools.partial(jnp.dot, preferred_element_type=F32)
    ya = dot(oa_ref[...], wa_ref[...])
    yr = dot(or_ref[...], wr_ref[...])
    mix = _sigmoid(ga_ref[...].astype(F32)) * ya + _sigmoid(gr_ref[...].astype(F32)) * yr
    h = x_ref[...] + dot(mix.astype(BF16), wo_ref[...])
    h_ref[...] = h
    xn = _rms(h) * gf_ref[...]
    xn_ref[...] = _pack_bf16_pair(xn[:, :HALF], xn[:, HALF:])
    x_hi, x_lo = _split_hi_lo(xn)
    w_hi, w_lo = _split_hi_lo(wrt_ref[...])
    lg_ref[...] = dot(x_hi, w_hi) + (dot(x_hi, w_lo) + dot(x_lo, w_hi)) + brt_ref[...]


def _mix_out(o_att, o_rec, hz, x2, w_up_att, w_up_rec, w_out, g_ffn, w_router, b_router, tm):
    n = x2.shape[0]
    const = lambda shape: pl.BlockSpec(shape, lambda i: (0,) * len(shape))
    ga_col = 4 * HGRN_W // D_MODEL
    return pl.pallas_call(
        _mix_kernel,
        grid=(n // tm,),
        in_specs=[
            pl.BlockSpec((tm, ATT_Q_W), lambda i: (i, 0)),
            pl.BlockSpec((tm, HGRN_W), lambda i: (i, 0)),
            pl.BlockSpec((tm, D_MODEL), lambda i: (i, ga_col)),
            pl.BlockSpec((tm, D_MODEL), lambda i: (i, ga_col + 1)),
            pl.BlockSpec((tm, D_MODEL), lambda i: (i, 0)),
            const((ATT_Q_W, D_MODEL)),
            const((HGRN_W, D_MODEL)),
            const((D_MODEL, D_MODEL)),
            const((1, D_MODEL)),
            const((D_MODEL, ROUTER_W)),
            const((1, ROUTER_W)),
        ],
        out_specs=[
            pl.BlockSpec((tm, D_MODEL), lambda i: (i, 0)),
            pl.BlockSpec((tm, HALF), lambda i: (i, 0)),
            pl.BlockSpec((tm, ROUTER_W), lambda i: (i, 0)),
        ],
        out_shape=[
            jax.ShapeDtypeStruct((n, D_MODEL), F32),
            jax.ShapeDtypeStruct((n, HALF), U32),
            jax.ShapeDtypeStruct((n, ROUTER_W), F32),
        ],
        compiler_params=_cparams("parallel"),
        name="mix_out",
    )(o_att, o_rec, hz, hz, x2, w_up_att, w_up_rec, w_out, g_ffn, w_router, b_router)


def _route_choice(lg):
    tm = lg.shape[0]
    lane = lax.broadcasted_iota(I32, lg.shape, 1)
    neg = jnp.float32(-jnp.inf)
    is_coarse = lane < MOE_GROUPS
    lc = jnp.where(is_coarse, lg, neg)
    cmax = jnp.max(lc, axis=-1, keepdims=True)
    p_sel = 1.0 / jnp.sum(jnp.exp(lc - cmax), axis=-1, keepdims=True)
    grp = jnp.min(jnp.where(lc == cmax, lane, ROUTER_W), axis=-1, keepdims=True)
    fine_lane = lane - MOE_GROUPS
    in_grp = (fine_lane >= grp * MOE_EXPERTS_PER_GROUP) & (fine_lane < (grp + 1) * MOE_EXPERTS_PER_GROUP)
    lf = jnp.where(in_grp, lg, neg)
    v0 = jnp.max(lf, axis=-1, keepdims=True)
    e0 = jnp.min(jnp.where(lf == v0, fine_lane, ROUTER_W), axis=-1, keepdims=True)
    lf1 = jnp.where(fine_lane == e0, neg, lf)
    v1 = jnp.max(lf1, axis=-1, keepdims=True)
    e1 = jnp.min(jnp.where(lf1 == v1, fine_lane, ROUTER_W), axis=-1, keepdims=True)
    t1 = jnp.exp(v1 - v0)
    w0 = p_sel / (1.0 + t1)
    w1 = p_sel * t1 / (1.0 + t1)
    return e0, e1, w0, w1


def _route_kernel(lg_ref, dest_ref, wts_ref, blk_ref, cnt_ref, *, tm, nblk_pad):
    phase = pl.program_id(0)
    i = pl.program_id(1)
    E = MOE_EXPERTS

    @pl.when((phase == 0) & (i == 0))
    def _():
        cnt_ref[...] = jnp.zeros_like(cnt_ref)

    e0, e1, w0, w1 = _route_choice(lg_ref[...])
    elane = lax.broadcasted_iota(I32, (tm, E), 1)
    oh0 = elane == e0
    oh1 = elane == e1
    picks = (oh0 | oh1).astype(F32)

    @pl.when(phase == 0)
    def _():
        cnt_ref[0:1, :] = cnt_ref[0:1, :] + jnp.sum(picks, axis=0, keepdims=True)

    @pl.when(phase == 1)
    def _():
        dot = functools.partial(jnp.dot, preferred_element_type=F32)
        nb = jnp.floor((cnt_ref[0:1, :] + (EXPERT_ROWS - 1)) * (1.0 / EXPERT_ROWS))
        nb_hi = jnp.floor(nb * (1.0 / 32.0))
        nb_lo = nb - 32.0 * nb_hi
        r = lax.broadcasted_iota(I32, (E, E), 0)
        c = lax.broadcasted_iota(I32, (E, E), 1)
        upper = (r < c).astype(F32).astype(BF16)
        incl = (r <= c).astype(F32).astype(BF16)
        nb8_hi = jnp.broadcast_to(nb_hi, (8, E)).astype(BF16)
        nb8_lo = jnp.broadcast_to(nb_lo, (8, E)).astype(BF16)
        start_blk = 32.0 * dot(nb8_hi, upper) + dot(nb8_lo, upper)
        end_blk = 32.0 * dot(nb8_hi, incl) + dot(nb8_lo, incl)

        @pl.when(i == 0)
        def _():
            cnt_ref[1:2, :] = jnp.zeros((1, E), F32)
            bidx = lax.broadcasted_iota(I32, (nblk_pad, E), 0).astype(F32)
            be = jnp.sum((end_blk[0:1, :] <= bidx).astype(F32), axis=-1, keepdims=True)
            be = jnp.minimum(be, float(E - 1))
            lane = lax.broadcasted_iota(I32, blk_ref.shape, 1)
            blk_ref[...] = jnp.where(lane == 1, end_blk[0:1, E - 1:E], be).astype(I32)

        tr = lax.broadcasted_iota(I32, (tm, tm), 0)
        tc = lax.broadcasted_iota(I32, (tm, tm), 1)
        strict = (tc < tr).astype(F32).astype(BF16)
        before = dot(strict, picks.astype(BF16))
        pos = start_blk[0:1, :] * float(EXPERT_ROWS) + cnt_ref[1:2, :] + before
        d0 = jnp.sum(jnp.where(oh0, pos, 0.0), axis=-1, keepdims=True)
        d1 = jnp.sum(jnp.where(oh1, pos, 0.0), axis=-1, keepdims=True)
        cnt_ref[1:2, :] = cnt_ref[1:2, :] + jnp.sum(picks, axis=0, keepdims=True)
        two = lax.broadcasted_iota(I32, (tm, 2), 1)
        dest_ref[...] = jnp.where(two == 0, d0, d1).astype(I32)
        wts_ref[...] = jnp.where(two == 0, w0, w1)


def _route(logits, tm, nblk_pad):
    n = logits.shape[0]
    return pl.pallas_call(
        functools.partial(_route_kernel, tm=tm, nblk_pad=nblk_pad),
        grid=(2, n // tm),
        in_specs=[pl.BlockSpec((tm, ROUTER_W), lambda p, i: (i, 0))],
        out_specs=[
            pl.BlockSpec((tm, 2), lambda p, i: (i * p, 0)),
            pl.BlockSpec((tm, 2), lambda p, i: (i * p, 0)),
            pl.BlockSpec((nblk_pad, 128), lambda p, i: (0, 0)),
        ],
        out_shape=[
            jax.ShapeDtypeStruct((n, 2), I32),
            jax.ShapeDtypeStruct((n, 2), F32),
            jax.ShapeDtypeStruct((nblk_pad, 128), I32),
        ],
        scratch_shapes=[pltpu.VMEM((8, MOE_EXPERTS), F32)],
        compiler_params=_cparams("arbitrary", "arbitrary"),
        name="route",
    )(logits)


def _dispatch_kernel(dest_ref, xn_ref, xs_in_ref, xs_ref, sem, *, tm):
    del xs_in_ref
    base = pl.program_id(0) * tm

    def copy(t, k):
        return pltpu.make_async_copy(xn_ref.at[pl.ds(base + t, 1), :],
                                     xs_ref.at[pl.ds(dest_ref[2 * (base + t) + k], 1), :], sem)

    def issue(t, c):
        copy(t, 0).start()
        copy(t, 1).start()
        return c

    def drain(t, c):
        copy(t, 0).wait()
        copy(t, 1).wait()
        return c

    lax.fori_loop(0, tm, issue, 0)
    lax.fori_loop(0, tm, drain, 0)


def _dispatch(dest_flat, xn_packed, cap, tm):
    n = xn_packed.shape[0]
    zeros = jnp.zeros((cap, HALF), U32)
    return pl.pallas_call(
        functools.partial(_dispatch_kernel, tm=tm),
        grid_spec=pltpu.PrefetchScalarGridSpec(
            num_scalar_prefetch=1,
            grid=(n // tm,),
            in_specs=[pl.BlockSpec(memory_space=pl.ANY), pl.BlockSpec(memory_space=pl.ANY)],
            out_specs=pl.BlockSpec(memory_space=pl.ANY),
            scratch_shapes=[pltpu.SemaphoreType.DMA],
        ),
        out_shape=jax.ShapeDtypeStruct((cap, HALF), U32),
        input_output_aliases={2: 0},
        compiler_params=_cparams("arbitrary"),
        name="dispatch",
    )(dest_flat, xn_packed, zeros)


def _expert_kernel(blk_ref, nused_ref, xs_ref, wg_ref, wu_ref, wd_ref, y_ref):
    active = pl.program_id(0) < nused_ref[0]

    @pl.when(jnp.logical_not(active))
    def _():
        y_ref[...] = jnp.zeros_like(y_ref)

    @pl.when(active)
    def _():
        dot = functools.partial(jnp.dot, preferred_element_type=F32)
        lo, hi = _unpack_bf16_pair(xs_ref[...])
        lo = lo.astype(BF16)
        hi = hi.astype(BF16)
        g = dot(lo, wg_ref[0, :HALF, :]) + dot(hi, wg_ref[0, HALF:, :])
        u = dot(lo, wu_ref[0, :HALF, :]) + dot(hi, wu_ref[0, HALF:, :])
        hid = (_silu(g) * u).astype(BF16)
        y = dot(hid, wd_ref[0])
        y_ref[...] = _pack_bf16_pair(y[:, :HALF], y[:, HALF:])


def _experts(blk_e, n_used, xs, w_gate, w_up, w_down):
    cap = xs.shape[0]
    nblk = cap // EXPERT_ROWS
    row = lambda i, blk, nu: (jnp.minimum(i, nu[0] - 1), 0)
    wsel = lambda i, blk, nu: (blk[jnp.minimum(i, nu[0] - 1)], 0, 0)
    return pl.pallas_call(
        _expert_kernel,
        grid_spec=pltpu.PrefetchScalarGridSpec(
            num_scalar_prefetch=2,
            grid=(nblk,),
            in_specs=[
                pl.BlockSpec((EXPERT_ROWS, HALF), row),
                pl.BlockSpec((1, D_MODEL, MOE_HIDDEN), wsel),
                pl.BlockSpec((1, D_MODEL, MOE_HIDDEN), wsel),
                pl.BlockSpec((1, MOE_HIDDEN, D_MODEL), wsel),
            ],
            out_specs=pl.BlockSpec((EXPERT_ROWS, HALF), lambda i, blk, nu: (i, 0)),
        ),
        out_shape=jax.ShapeDtypeStruct((cap, HALF), U32),
        compiler_params=_cparams("arbitrary"),
        name="experts",
    )(blk_e, n_used, xs, w_gate, w_up, w_down)


def _combine_kernel(dest_ref, h_ref, wts_ref, p_ref, y_ref, wpi_ref, gp_ref, wpg_ref, gfin_ref, o_ref,
                    ybuf, sem, *, tm):
    base = pl.program_id(0) * tm

    def copy(t, k):
        return pltpu.make_async_copy(y_ref.at[pl.ds(dest_ref[2 * (base + t) + k], 1), :],
                                     ybuf.at[k, pl.ds(t, 1), :], sem)

    def issue(t, c):
        copy(t, 0).start()
        copy(t, 1).start()
        return c

    def drain(t, c):
        copy(t, 0).wait()
        copy(t, 1).wait()
        return c

    lax.fori_loop(0, tm, issue, 0)
    dot = functools.partial(jnp.dot, preferred_element_type=F32)
    ple = _rms(dot(p_ref[...].astype(BF16), wpi_ref[...])) * gp_ref[...]
    lax.fori_loop(0, tm, drain, 0)

    w = wts_ref[...]
    w0 = w[:, 0:1]
    w1 = w[:, 1:2]
    lo0, hi0 = _unpack_bf16_pair(ybuf[0])
    lo1, hi1 = _unpack_bf16_pair(ybuf[1])
    moe = jnp.concatenate([w0 * lo0 + w1 * lo1, w0 * hi0 + w1 * hi1], axis=-1)
    h = h_ref[...] + moe
    gate = _sigmoid(dot(_rms(h).astype(BF16), wpg_ref[...]))
    h = h + ple * gate
    o_ref[...] = _rms(h) * gfin_ref[...]


def _combine_ple(dest_flat, h1, wts, p2, y, w_ple_in, g_ple, w_ple_gate, g_final, tm):
    n = h1.shape[0]
    const = lambda shape: pl.BlockSpec(shape, lambda i, d: (0,) * len(shape))
    return pl.pallas_call(
        functools.partial(_combine_kernel, tm=tm),
        grid_spec=pltpu.PrefetchScalarGridSpec(
            num_scalar_prefetch=1,
            grid=(n // tm,),
            in_specs=[
                pl.BlockSpec((tm, D_MODEL), lambda i, d: (i, 0)),
                pl.BlockSpec((tm, 2), lambda i, d: (i, 0)),
                pl.BlockSpec((tm, PLE_DIM), lambda i, d: (i, 0)),
                pl.BlockSpec(memory_space=pl.ANY),
                const((PLE_DIM, D_MODEL)),
                const((1, D_MODEL)),
                const((D_MODEL, D_MODEL)),
                const((1, D_MODEL)),
            ],
            out_specs=pl.BlockSpec((tm, D_MODEL), lambda i, d: (i, 0)),
            scratch_shapes=[pltpu.VMEM((2, tm, HALF), U32), pltpu.SemaphoreType.DMA],
        ),
        out_shape=jax.ShapeDtypeStruct((n, D_MODEL), F32),
        compiler_params=_cparams("arbitrary"),
        name="combine_ple",
    )(dest_flat, h1, wts, p2, y, w_ple_in, g_ple, w_ple_gate, g_final)


def _tile(n, pref):
    t = min(n, pref)
    assert n % t == 0, (n, t)
    return t


def kernel(x, p, g_mix, w_in, b_qkv, sinks, rel_bias, lb_logits, g_hgrn, w_up_att, w_up_rec, w_out, g_ffn,
           w_coarse, b_coarse, w_fine, b_fine, w_gate, w_up, w_down, w_ple_in, g_ple, w_ple_gate, g_final):
    B, T, D = x.shape
    assert D == D_MODEL and T % ATT_BLOCK == 0 and lb_logits.shape[0] == 2
    n = B * T
    layer = 0
    x2 = x.reshape(n, D)
    row = lambda a: a.reshape(1, -1)

    w_in_b = w_in[layer].astype(BF16)
    qkv = _qkv_proj(x2, row(g_mix[layer]), w_in_b[:, :ATT_QKV_W], row(b_qkv[layer]), _tile(n, 512))
    hz, f_raw = _hz_proj(x2, row(g_mix[layer]), w_in_b[:, ATT_QKV_W:], _tile(n, 1024))

    o_att = _swa(qkv, _bias_table(rel_bias), row(sinks[layer]), B, T)
    o_rec = _hgrn(hz, f_raw, lb_logits, row(g_hgrn[layer]), B, T, _tile(T, 512))

    w_router = jnp.concatenate(
        [w_coarse[layer], w_fine[layer], jnp.zeros((D, ROUTER_W - MOE_GROUPS - MOE_EXPERTS), F32)], axis=1)
    b_router = jnp.concatenate(
        [b_coarse[layer], b_fine[layer], jnp.zeros((ROUTER_W - MOE_GROUPS - MOE_EXPERTS,), F32)]).reshape(1, -1)
    h1, xn_packed, logits = _mix_out(
        o_att, o_rec, hz, x2, w_up_att[layer].astype(BF16), w_up_rec[layer].astype(BF16),
        w_out[layer].astype(BF16), row(g_ffn[layer]), w_router, b_router, _tile(n, 256))

    cap = 2 * n + MOE_EXPERTS * EXPERT_ROWS
    nblk = cap // EXPERT_ROWS
    nblk_pad = (nblk + 7) // 8 * 8
    dest, wts, blk_tab = _route(logits, _tile(n, 512), nblk_pad)
    dest_flat = dest.reshape(2 * n)
    blk_e = blk_tab[:nblk, 0]
    n_used = blk_tab[0, 1].reshape(1)

    xs = _dispatch(dest_flat, xn_packed, cap, _tile(n, 512))
    y = _experts(blk_e, n_used, xs, w_gate[layer].astype(BF16), w_up[layer].astype(BF16),
                 w_down[layer].astype(BF16))
    out = _combine_ple(dest_flat, h1, wts, p[layer].reshape(n, PLE_DIM), y, w_ple_in[layer].astype(BF16),
                       row(g_ple[layer]), w_ple_gate[layer].astype(BF16), row(g_final), _tile(n, 256))
    return out.reshape(B, T, D)
```

```python
import functools
import math

import numpy as np
import jax
import jax.numpy as jnp
from jax import lax
from jax.experimental import pallas as pl
from jax.experimental.pallas import tpu as pltpu

F32 = jnp.float32
BF16 = jnp.bfloat16
I32 = jnp.int32
U32 = jnp.uint32

D_MODEL = 2048
ATT_Q_HEADS = 16
ATT_KV_HEADS = 2
ATT_GROUP = ATT_Q_HEADS // ATT_KV_HEADS
ATT_HEAD_DIM = 64
WINDOW = 128
ATT_BLOCK = 128
ATT_Q_W = ATT_Q_HEADS * ATT_HEAD_DIM
ATT_KV_W = ATT_KV_HEADS * ATT_HEAD_DIM
ATT_QKV_W = ATT_Q_W + 2 * ATT_KV_W
REL_BUCKETS = 32
REL_MAX_DIST = 128
HGRN_HEADS = 8
HGRN_DK = 128
HGRN_DV = 128
HGRN_W = HGRN_HEADS * HGRN_DV
HGRN_CHUNK = 64
MOE_GROUPS = 8
MOE_EXPERTS_PER_GROUP = 8
MOE_EXPERTS = MOE_GROUPS * MOE_EXPERTS_PER_GROUP
MOE_HIDDEN = 512
PLE_DIM = 256
EPS = 1e-6

HALF = D_MODEL // 2
ROUTER_W = 128
EXPERT_ROWS = 256
VMEM_LIMIT = 56 * 1024 * 1024


def _cparams(*sem):
    return pltpu.CompilerParams(dimension_semantics=sem, vmem_limit_bytes=VMEM_LIMIT)


def _rms(xf):
    return xf * lax.rsqrt(jnp.mean(xf * xf, axis=-1, keepdims=True) + EPS)


def _sigmoid(x):
    return 0.5 + 0.5 * jnp.tanh(0.5 * x)


def _silu(x):
    h = 0.5 * x
    return h + h * jnp.tanh(h)


def _pack_bf16_pair(lo_f32, hi_f32):
    lo = lax.bitcast_convert_type(lo_f32.astype(BF16).astype(F32), U32)
    hi = lax.bitcast_convert_type(hi_f32.astype(BF16).astype(F32), U32)
    return hi | (lo >> 16)


def _unpack_bf16_pair(w):
    lo = lax.bitcast_convert_type(w << 16, F32)
    hi = lax.bitcast_convert_type(w & jnp.uint32(0xFFFF0000), F32)
    return lo, hi


LANES = 128
ROW_TILE = HALF // LANES


def _store_token_rows(ref, words):
    tokens = words.shape[0]
    for c in range(ROW_TILE):
        ref[pl.ds(c, tokens, stride=ROW_TILE), :] = words[:, c * LANES:(c + 1) * LANES]


def _load_token_rows(ref, tokens):
    return jnp.concatenate([ref[pl.ds(c, tokens, stride=ROW_TILE), :] for c in range(ROW_TILE)], axis=-1)


def _token_row(ref, idx):
    return ref.at[pl.ds(pl.multiple_of(idx * ROW_TILE, ROW_TILE), ROW_TILE), :]


def _qkv_kernel(x_ref, g_ref, w_ref, b_ref, o_ref):
    u = (_rms(x_ref[...]) * g_ref[...]).astype(BF16)
    acc = jnp.dot(u, w_ref[...], preferred_element_type=F32)
    o_ref[...] = (acc + b_ref[...]).astype(o_ref.dtype)


def _qkv_proj(x2, g_mix, w_qkv, b_qkv, tm):
    n = x2.shape[0]
    return pl.pallas_call(
        _qkv_kernel,
        grid=(n // tm,),
        in_specs=[
            pl.BlockSpec((tm, D_MODEL), lambda i: (i, 0)),
            pl.BlockSpec((1, D_MODEL), lambda i: (0, 0)),
            pl.BlockSpec((D_MODEL, ATT_QKV_W), lambda i: (0, 0)),
            pl.BlockSpec((1, ATT_QKV_W), lambda i: (0, 0)),
        ],
        out_specs=pl.BlockSpec((tm, ATT_QKV_W), lambda i: (i, 0)),
        out_shape=jax.ShapeDtypeStruct((n, ATT_QKV_W), BF16),
        compiler_params=_cparams("parallel"),
        name="qkv_proj",
    )(x2, g_mix, w_qkv, b_qkv)


HZ_TN = 1024
HZ_COLS = 4 * HGRN_W + 2 * D_MODEL
HZ_F_BLOCK = 1


def _hz_kernel(x_ref, g_ref, w_ref, hz_ref, f_ref, u_scr):
    j = pl.program_id(1)

    @pl.when(j == 0)
    def _():
        u_scr[...] = (_rms(x_ref[...]) * g_ref[...]).astype(BF16)

    acc = jnp.dot(u_scr[...], w_ref[...], preferred_element_type=F32)
    hz_ref[...] = acc.astype(BF16)

    @pl.when(j == HZ_F_BLOCK)
    def _():
        f_ref[...] = acc


def _hz_proj(x2, g_mix, w_hz, tm):
    n = x2.shape[0]
    return pl.pallas_call(
        _hz_kernel,
        grid=(n // tm, HZ_COLS // HZ_TN),
        in_specs=[
            pl.BlockSpec((tm, D_MODEL), lambda i, j: (i, 0)),
            pl.BlockSpec((1, D_MODEL), lambda i, j: (0, 0)),
            pl.BlockSpec((D_MODEL, HZ_TN), lambda i, j: (0, j)),
        ],
        out_specs=[
            pl.BlockSpec((tm, HZ_TN), lambda i, j: (i, j)),
            pl.BlockSpec((tm, HGRN_W), lambda i, j: (i, 0)),
        ],
        out_shape=[
            jax.ShapeDtypeStruct((n, HZ_COLS), BF16),
            jax.ShapeDtypeStruct((n, HGRN_W), F32),
        ],
        scratch_shapes=[pltpu.VMEM((tm, D_MODEL), BF16)],
        compiler_params=_cparams("parallel", "arbitrary"),
        name="hz_proj",
    )(x2, g_mix, w_hz)


def _t5_bucket_table():
    qi = np.arange(ATT_BLOCK)[:, None]
    kj = np.arange(2 * ATT_BLOCK)[None, :]
    dist = qi + ATT_BLOCK - kj
    exact = REL_BUCKETS // 2
    d = np.maximum(dist, 0)
    large = exact + (np.log(np.maximum(d, 1).astype(np.float32) / exact)
                     / math.log(REL_MAX_DIST / exact) * (REL_BUCKETS - exact)).astype(np.int32)
    large = np.minimum(large, REL_BUCKETS - 1)
    return np.where(d < exact, d, large).astype(np.int32)


def _bias_kernel(bucket_ref, rel_ref, o_ref):
    bk = bucket_ref[...]
    qi = lax.broadcasted_iota(I32, bk.shape, 0)
    kj = lax.broadcasted_iota(I32, bk.shape, 1)
    dist = qi + ATT_BLOCK - kj
    in_window = (dist >= 0) & (dist < WINDOW)
    first_block = in_window & (kj >= ATT_BLOCK)
    neg = jnp.float32(-jnp.inf)
    for h in range(ATT_Q_HEADS):
        acc = jnp.zeros(bk.shape, F32)
        for b in range(REL_BUCKETS):
            acc = jnp.where(bk == b, rel_ref[b, h], acc)
        o_ref[0, h] = jnp.where(first_block, acc, neg)
        o_ref[1, h] = jnp.where(in_window, acc, neg)


def _bias_table(rel_bias):
    bucket = jnp.asarray(_t5_bucket_table())
    return pl.pallas_call(
        _bias_kernel,
        in_specs=[
            pl.BlockSpec(memory_space=pltpu.VMEM),
            pl.BlockSpec(memory_space=pltpu.SMEM),
        ],
        out_specs=pl.BlockSpec(memory_space=pltpu.VMEM),
        out_shape=jax.ShapeDtypeStruct((2, ATT_Q_HEADS, ATT_BLOCK, 2 * ATT_BLOCK), F32),
        name="bias_table",
    )(bucket, rel_bias)


def _swa_kernel(q_ref, kvc_ref, kvp_ref, bias_ref, sink_ref, o_ref):
    hd = ATT_HEAD_DIM
    scale = hd ** -0.5
    for kk in range(ATT_KV_HEADS):
        k = jnp.concatenate([kvp_ref[:, hd * kk:hd * (kk + 1)], kvc_ref[:, hd * kk:hd * (kk + 1)]], axis=0)
        v = jnp.concatenate([kvp_ref[:, ATT_KV_W + hd * kk:ATT_KV_W + hd * (kk + 1)],
                             kvc_ref[:, ATT_KV_W + hd * kk:ATT_KV_W + hd * (kk + 1)]], axis=0)
        for g in range(ATT_GROUP):
            h = kk * ATT_GROUP + g
            q = q_ref[:, hd * h:hd * (h + 1)]
            s = lax.dot_general(q, k, (((1,), (1,)), ((), ())), preferred_element_type=F32)
            s = s * scale + bias_ref[0, h]
            sink = sink_ref[0, h]
            m = jnp.maximum(jnp.max(s, axis=-1, keepdims=True), sink)
            p = jnp.exp(s - m)
            denom = jnp.sum(p, axis=-1, keepdims=True) + jnp.exp(sink - m)
            o = jnp.dot(p.astype(BF16), v, preferred_element_type=F32) / denom
            o_ref[:, hd * h:hd * (h + 1)] = o.astype(o_ref.dtype)


def _swa(qkv, bias_tab, sinks, batch, seq):
    n = qkv.shape[0]
    nb = seq // ATT_BLOCK
    kv_col = ATT_Q_W // (2 * ATT_KV_W)

    return pl.pallas_call(
        _swa_kernel,
        grid=(batch, nb),
        in_specs=[
            pl.BlockSpec((ATT_BLOCK, ATT_Q_W), lambda b, i: (b * nb + i, 0)),
            pl.BlockSpec((ATT_BLOCK, 2 * ATT_KV_W), lambda b, i: (b * nb + i, kv_col)),
            pl.BlockSpec((ATT_BLOCK, 2 * ATT_KV_W), lambda b, i: (b * nb + jnp.maximum(i - 1, 0), kv_col)),
            pl.BlockSpec((1, ATT_Q_HEADS, ATT_BLOCK, 2 * ATT_BLOCK), lambda b, i: (jnp.minimum(i, 1), 0, 0, 0)),
            pl.BlockSpec(memory_space=pltpu.SMEM),
        ],
        out_specs=pl.BlockSpec((ATT_BLOCK, ATT_Q_W), lambda b, i: (b * nb + i, 0)),
        out_shape=jax.ShapeDtypeStruct((n, ATT_Q_W), BF16),
        compiler_params=_cparams("parallel", "arbitrary"),
        name="swa",
    )(qkv, qkv, qkv, bias_tab, sinks)


def _cumsum_rows(tri_bf16, g):
    w = g.shape[1]
    g1 = g.astype(BF16)
    r1 = g - g1.astype(F32)
    g2 = r1.astype(BF16)
    g3 = (r1 - g2.astype(F32)).astype(BF16)
    parts = jnp.dot(tri_bf16, jnp.concatenate([g1, g2, g3], axis=1), preferred_element_type=F32)
    return parts[:, :w] + (parts[:, w:2 * w] + parts[:, 2 * w:])


def _hgrn_kernel(q_ref, f_ref, i_ref, g_ref, lbl_ref, gh_ref, o_ref, st_ref, *, chunks):
    C = HGRN_CHUNK
    ref_row = C // 2 - 1

    @pl.when(pl.program_id(1) == 0)
    def _():
        st_ref[...] = jnp.zeros_like(st_ref)

    l = lbl_ref[...]
    e = jnp.exp(l - jnp.max(l, axis=0, keepdims=True))
    lb_all = e[0:1, :] / jnp.sum(e, axis=0, keepdims=True)

    row = lax.broadcasted_iota(I32, (C, C), 0)
    col = lax.broadcasted_iota(I32, (C, C), 1)
    causal = row >= col
    tri = causal.astype(F32).astype(BF16)
    gain = gh_ref[...]

    def chunk(c, carry):
        r0 = pl.multiple_of(c * C, C)
        rows = pl.ds(r0, C)
        f = lb_all + (1.0 - lb_all) * _sigmoid(f_ref[rows, :])
        k = 1.0 - f
        b = _cumsum_rows(tri, jnp.log(f))
        qs = _silu(q_ref[rows, :].astype(F32))
        b_ref = b[ref_row:ref_row + 1, :]
        b_last = b[C - 1:C, :]
        qe_all = (qs * jnp.exp(b)).astype(BF16)
        qa_all = (qs * jnp.exp(b - b_ref)).astype(BF16)
        ka_all = (k * jnp.exp(b_ref - b)).astype(BF16)
        kl_all = (k * jnp.exp(b_last - b)).astype(BF16)
        decay = jnp.exp(b_last)
        gate = _silu(g_ref[rows, :].astype(F32))
        nt = (((1,), (1,)), ((), ()))
        heads = [slice(HGRN_DK * h, HGRN_DK * (h + 1)) for h in range(HGRN_HEADS)]
        vs = [i_ref[rows, cs] for cs in heads]
        sts = [st_ref[h] for h in range(HGRN_HEADS)]
        attn = [lax.dot_general(qa_all[:, cs], ka_all[:, cs], nt, preferred_element_type=F32) for cs in heads]
        inter = [lax.dot_general(qe_all[:, cs], st.astype(BF16), nt, preferred_element_type=F32)
                 for cs, st in zip(heads, sts)]
        upd = [lax.dot_general(v, kl_all[:, cs], (((0,), (0,)), ((), ())), preferred_element_type=F32)
               for cs, v in zip(heads, vs)]
        for h, cs in enumerate(heads):
            st_ref[h] = sts[h] * decay[:, cs] + upd[h]
        intra = [jnp.dot(jnp.where(causal, a, 0.0).astype(BF16), v, preferred_element_type=F32)
                 for a, v in zip(attn, vs)]
        for h, cs in enumerate(heads):
            o = inter[h] + intra[h]
            o_ref[rows, cs] = (_rms(o) * gain * gate[:, cs]).astype(o_ref.dtype)
        return carry

    lax.fori_loop(0, chunks, chunk, 0, unroll=2)


def _hgrn(hz, f_raw, lb_logits, g_hgrn, batch, seq, tc):
    n = hz.shape[0]
    nt = seq // tc
    blk = lambda col: pl.BlockSpec((tc, HGRN_W), lambda b, t, col=col: (b * nt + t, col))
    return pl.pallas_call(
        functools.partial(_hgrn_kernel, chunks=tc // HGRN_CHUNK),
        grid=(batch, nt),
        in_specs=[
            blk(0),
            pl.BlockSpec((tc, HGRN_W), lambda b, t: (b * nt + t, 0)),
            blk(2),
            blk(3),
            pl.BlockSpec((2, HGRN_W), lambda b, t: (0, 0)),
            pl.BlockSpec((1, HGRN_DV), lambda b, t: (0, 0)),
        ],
        out_specs=pl.BlockSpec((tc, HGRN_W), lambda b, t: (b * nt + t, 0)),
        out_shape=jax.ShapeDtypeStruct((n, HGRN_W), BF16),
        scratch_shapes=[pltpu.VMEM((HGRN_HEADS, HGRN_DV, HGRN_DK), F32)],
        compiler_params=_cparams("parallel", "arbitrary"),
        name="hgrn",
    )(hz, f_raw, hz, hz, lb_logits, g_hgrn)


def _split_hi_lo(a):
    hi = a.astype(BF16)
    lo = (a - hi.astype(F32)).astype(BF16)
    return hi, lo


def _mix_kernel(oa_ref, or_ref, ga_ref, gr_ref, x_ref, wa_ref, wr_ref, wo_ref, gf_ref, wrt_ref, brt_ref,
                h_ref, xn_ref, lg_ref):
    dot = functools.partial(jnp.dot, preferred_element_type=F32)
    ya = dot(oa_ref[...], wa_ref[...])
    yr = dot(or_ref[...], wr_ref[...])
    mix = _sigmoid(ga_ref[...].astype(F32)) * ya + _sigmoid(gr_ref[...].astype(F32)) * yr
    h = x_ref[...] + dot(mix.astype(BF16), wo_ref[...])
    h_ref[...] = h
    xn = _rms(h) * gf_ref[...]
    _store_token_rows(xn_ref, _pack_bf16_pair(xn[:, :HALF], xn[:, HALF:]))
    x_hi, x_lo = _split_hi_lo(xn)
    w_hi, w_lo = _split_hi_lo(wrt_ref[...])
    lg_ref[...] = dot(x_hi, w_hi) + (dot(x_hi, w_lo) + dot(x_lo, w_hi)) + brt_ref[...]


def _mix_out(o_att, o_rec, hz, x2, w_up_att, w_up_rec, w_out, g_ffn, w_router, b_router, tm):
    n = x2.shape[0]
    const = lambda shape: pl.BlockSpec(shape, lambda i: (0,) * len(shape))
    ga_col = 4 * HGRN_W // D_MODEL
    return pl.pallas_call(
        _mix_kernel,
        grid=(n // tm,),
        in_specs=[
            pl.BlockSpec((tm, ATT_Q_W), lambda i: (i, 0)),
            pl.BlockSpec((tm, HGRN_W), lambda i: (i, 0)),
            pl.BlockSpec((tm, D_MODEL), lambda i: (i, ga_col)),
            pl.BlockSpec((tm, D_MODEL), lambda i: (i, ga_col + 1)),
            pl.BlockSpec((tm, D_MODEL), lambda i: (i, 0)),
            const((ATT_Q_W, D_MODEL)),
            const((HGRN_W, D_MODEL)),
            const((D_MODEL, D_MODEL)),
            const((1, D_MODEL)),
            const((D_MODEL, ROUTER_W)),
            const((1, ROUTER_W)),
        ],
        out_specs=[
            pl.BlockSpec((tm, D_MODEL), lambda i: (i, 0)),
            pl.BlockSpec((tm * ROW_TILE, LANES), lambda i: (i, 0)),
            pl.BlockSpec((tm, ROUTER_W), lambda i: (i, 0)),
        ],
        out_shape=[
            jax.ShapeDtypeStruct((n, D_MODEL), F32),
            jax.ShapeDtypeStruct((n * ROW_TILE, LANES), U32),
            jax.ShapeDtypeStruct((n, ROUTER_W), F32),
        ],
        compiler_params=_cparams("parallel"),
        name="mix_out",
    )(o_att, o_rec, hz, hz, x2, w_up_att, w_up_rec, w_out, g_ffn, w_router, b_router)


def _route_choice(lg):
    tm = lg.shape[0]
    lane = lax.broadcasted_iota(I32, lg.shape, 1)
    neg = jnp.float32(-jnp.inf)
    is_coarse = lane < MOE_GROUPS
    lc = jnp.where(is_coarse, lg, neg)
    cmax = jnp.max(lc, axis=-1, keepdims=True)
    p_sel = 1.0 / jnp.sum(jnp.exp(lc - cmax), axis=-1, keepdims=True)
    grp = jnp.min(jnp.where(lc == cmax, lane, ROUTER_W), axis=-1, keepdims=True)
    fine_lane = lane - MOE_GROUPS
    in_grp = (fine_lane >= grp * MOE_EXPERTS_PER_GROUP) & (fine_lane < (grp + 1) * MOE_EXPERTS_PER_GROUP)
    lf = jnp.where(in_grp, lg, neg)
    v0 = jnp.max(lf, axis=-1, keepdims=True)
    e0 = jnp.min(jnp.where(lf == v0, fine_lane, ROUTER_W), axis=-1, keepdims=True)
    lf1 = jnp.where(fine_lane == e0, neg, lf)
    v1 = jnp.max(lf1, axis=-1, keepdims=True)
    e1 = jnp.min(jnp.where(lf1 == v1, fine_lane, ROUTER_W), axis=-1, keepdims=True)
    t1 = jnp.exp(v1 - v0)
    w0 = p_sel / (1.0 + t1)
    w1 = p_sel * t1 / (1.0 + t1)
    return e0, e1, w0, w1


def _route_kernel(lg_ref, dest_ref, wts_ref, blk_ref, cnt_ref, *, tm, nblk_pad):
    phase = pl.program_id(0)
    i = pl.program_id(1)
    E = MOE_EXPERTS

    @pl.when((phase == 0) & (i == 0))
    def _():
        cnt_ref[...] = jnp.zeros_like(cnt_ref)

    e0, e1, w0, w1 = _route_choice(lg_ref[...])
    elane = lax.broadcasted_iota(I32, (tm, E), 1)
    oh0 = elane == e0
    oh1 = elane == e1
    picks = (oh0 | oh1).astype(F32)

    @pl.when(phase == 0)
    def _():
        cnt_ref[0:1, :] = cnt_ref[0:1, :] + jnp.sum(picks, axis=0, keepdims=True)

    @pl.when(phase == 1)
    def _():
        dot = functools.partial(jnp.dot, preferred_element_type=F32)
        nb = jnp.floor((cnt_ref[0:1, :] + (EXPERT_ROWS - 1)) * (1.0 / EXPERT_ROWS))
        nb_hi = jnp.floor(nb * (1.0 / 32.0))
        nb_lo = nb - 32.0 * nb_hi
        r = lax.broadcasted_iota(I32, (E, E), 0)
        c = lax.broadcasted_iota(I32, (E, E), 1)
        upper = (r < c).astype(F32).astype(BF16)
        incl = (r <= c).astype(F32).astype(BF16)
        nb8_hi = jnp.broadcast_to(nb_hi, (8, E)).astype(BF16)
        nb8_lo = jnp.broadcast_to(nb_lo, (8, E)).astype(BF16)
        start_blk = 32.0 * dot(nb8_hi, upper) + dot(nb8_lo, upper)
        end_blk = 32.0 * dot(nb8_hi, incl) + dot(nb8_lo, incl)

        @pl.when(i == 0)
        def _():
            cnt_ref[1:2, :] = jnp.zeros((1, E), F32)
            bidx = lax.broadcasted_iota(I32, (nblk_pad, E), 0).astype(F32)
            be = jnp.sum((end_blk[0:1, :] <= bidx).astype(F32), axis=-1, keepdims=True)
            be = jnp.minimum(be, float(E - 1))
            lane = lax.broadcasted_iota(I32, blk_ref.shape, 1)
            blk_ref[...] = jnp.where(lane == 1, end_blk[0:1, E - 1:E], be).astype(I32)

        tr = lax.broadcasted_iota(I32, (tm, tm), 0)
        tc = lax.broadcasted_iota(I32, (tm, tm), 1)
        strict = (tc < tr).astype(F32).astype(BF16)
        before = dot(strict, picks.astype(BF16))
        pos = start_blk[0:1, :] * float(EXPERT_ROWS) + cnt_ref[1:2, :] + before
        d0 = jnp.sum(jnp.where(oh0, pos, 0.0), axis=-1, keepdims=True)
        d1 = jnp.sum(jnp.where(oh1, pos, 0.0), axis=-1, keepdims=True)
        cnt_ref[1:2, :] = cnt_ref[1:2, :] + jnp.sum(picks, axis=0, keepdims=True)
        two = lax.broadcasted_iota(I32, (tm, 2), 1)
        dest_ref[...] = jnp.where(two == 0, d0, d1).astype(I32)
        wts_ref[...] = jnp.where(two == 0, w0, w1)


def _route(logits, tm, nblk_pad):
    n = logits.shape[0]
    return pl.pallas_call(
        functools.partial(_route_kernel, tm=tm, nblk_pad=nblk_pad),
        grid=(2, n // tm),
        in_specs=[pl.BlockSpec((tm, ROUTER_W), lambda p, i: (i, 0))],
        out_specs=[
            pl.BlockSpec((tm, 2), lambda p, i: (i * p, 0)),
            pl.BlockSpec((tm, 2), lambda p, i: (i * p, 0)),
            pl.BlockSpec((nblk_pad, 128), lambda p, i: (0, 0)),
        ],
        out_shape=[
            jax.ShapeDtypeStruct((n, 2), I32),
            jax.ShapeDtypeStruct((n, 2), F32),
            jax.ShapeDtypeStruct((nblk_pad, 128), I32),
        ],
        scratch_shapes=[pltpu.VMEM((8, MOE_EXPERTS), F32)],
        compiler_params=_cparams("arbitrary", "arbitrary"),
        name="route",
    )(logits)


ISSUE_UNROLL = 8


def _dispatch_kernel(dest_ref, xn_ref, xs_in_ref, xs_ref, sem, *, tm):
    del xs_in_ref
    base = pl.program_id(0) * tm

    def issue(t, c):
        for k in range(2):
            pltpu.make_async_copy(_token_row(xn_ref, t),
                                  _token_row(xs_ref, dest_ref[2 * (base + t) + k]), sem).start()
        return c

    lax.fori_loop(0, tm, issue, 0, unroll=ISSUE_UNROLL)
    for k in range(2):
        pltpu.make_async_copy(xn_ref, xs_ref.at[pl.ds(0, tm * ROW_TILE), :], sem).wait()


def _dispatch(dest_flat, xn_packed, cap, tm):
    n = xn_packed.shape[0] // ROW_TILE
    zeros = jnp.zeros((cap * ROW_TILE, LANES), U32)
    return pl.pallas_call(
        functools.partial(_dispatch_kernel, tm=tm),
        grid_spec=pltpu.PrefetchScalarGridSpec(
            num_scalar_prefetch=1,
            grid=(n // tm,),
            in_specs=[pl.BlockSpec((tm * ROW_TILE, LANES), lambda i, d: (i, 0)),
                      pl.BlockSpec(memory_space=pl.ANY)],
            out_specs=pl.BlockSpec(memory_space=pl.ANY),
            scratch_shapes=[pltpu.SemaphoreType.DMA],
        ),
        out_shape=jax.ShapeDtypeStruct((cap * ROW_TILE, LANES), U32),
        input_output_aliases={2: 0},
        compiler_params=_cparams("arbitrary"),
        name="dispatch",
    )(dest_flat, xn_packed, zeros)


def _expert_kernel(blk_ref, nused_ref, xs_ref, wg_ref, wu_ref, wd_ref, y_ref):
    active = pl.program_id(0) < nused_ref[0]

    @pl.when(jnp.logical_not(active))
    def _():
        y_ref[...] = jnp.zeros_like(y_ref)

    @pl.when(active)
    def _():
        dot = functools.partial(jnp.dot, preferred_element_type=F32)
        lo, hi = _unpack_bf16_pair(_load_token_rows(xs_ref, EXPERT_ROWS))
        lo = lo.astype(BF16)
        hi = hi.astype(BF16)
        g = dot(lo, wg_ref[0, :HALF, :]) + dot(hi, wg_ref[0, HALF:, :])
        u = dot(lo, wu_ref[0, :HALF, :]) + dot(hi, wu_ref[0, HALF:, :])
        hid = (_silu(g) * u).astype(BF16)
        y = dot(hid, wd_ref[0])
        _store_token_rows(y_ref, _pack_bf16_pair(y[:, :HALF], y[:, HALF:]))


def _experts(blk_e, n_used, xs, w_gate, w_up, w_down):
    cap = xs.shape[0] // ROW_TILE
    nblk = cap // EXPERT_ROWS
    row = lambda i, blk, nu: (jnp.minimum(i, nu[0] - 1), 0)
    wsel = lambda i, blk, nu: (blk[jnp.minimum(i, nu[0] - 1)], 0, 0)
    return pl.pallas_call(
        _expert_kernel,
        grid_spec=pltpu.PrefetchScalarGridSpec(
            num_scalar_prefetch=2,
            grid=(nblk,),
            in_specs=[
                pl.BlockSpec((EXPERT_ROWS * ROW_TILE, LANES), row),
                pl.BlockSpec((1, D_MODEL, MOE_HIDDEN), wsel),
                pl.BlockSpec((1, D_MODEL, MOE_HIDDEN), wsel),
                pl.BlockSpec((1, MOE_HIDDEN, D_MODEL), wsel),
            ],
            out_specs=pl.BlockSpec((EXPERT_ROWS * ROW_TILE, LANES), lambda i, blk, nu: (i, 0)),
        ),
        out_shape=jax.ShapeDtypeStruct((cap * ROW_TILE, LANES), U32),
        compiler_params=_cparams("arbitrary"),
        name="experts",
    )(blk_e, n_used, xs, w_gate, w_up, w_down)


def _combine_kernel(dest_ref, h_ref, wts_ref, p_ref, y_ref, wpi_ref, gp_ref, wpg_ref, gfin_ref, o_ref,
                    ybuf, sems, *, tm):
    i = pl.program_id(0)
    slot = lax.rem(i, 2)

    def gather(tile, slot):
        base = tile * tm

        def issue(t, c):
            for k in range(2):
                pltpu.make_async_copy(_token_row(y_ref, dest_ref[2 * (base + t) + k]),
                                      _token_row(ybuf.at[slot, k], t), sems.at[slot]).start()
            return c

        lax.fori_loop(0, tm, issue, 0, unroll=ISSUE_UNROLL)

    @pl.when(i == 0)
    def _():
        gather(0, 0)

    @pl.when(i + 1 < pl.num_programs(0))
    def _():
        gather(i + 1, 1 - slot)

    dot = functools.partial(jnp.dot, preferred_element_type=F32)
    ple = _rms(dot(p_ref[...].astype(BF16), wpi_ref[...])) * gp_ref[...]

    for k in range(2):
        pltpu.make_async_copy(y_ref.at[pl.ds(0, tm * ROW_TILE), :], ybuf.at[slot, k], sems.at[slot]).wait()

    w = wts_ref[...]
    w0 = w[:, 0:1]
    w1 = w[:, 1:2]
    lo0, hi0 = _unpack_bf16_pair(_load_token_rows(ybuf.at[slot, 0], tm))
    lo1, hi1 = _unpack_bf16_pair(_load_token_rows(ybuf.at[slot, 1], tm))
    moe = jnp.concatenate([w0 * lo0 + w1 * lo1, w0 * hi0 + w1 * hi1], axis=-1)
    h = h_ref[...] + moe
    gate = _sigmoid(dot(_rms(h).astype(BF16), wpg_ref[...]))
    h = h + ple * gate
    o_ref[...] = _rms(h) * gfin_ref[...]


def _combine_ple(dest_flat, h1, wts, p2, y, w_ple_in, g_ple, w_ple_gate, g_final, tm):
    n = h1.shape[0]
    const = lambda shape: pl.BlockSpec(shape, lambda i, d: (0,) * len(shape))
    return pl.pallas_call(
        functools.partial(_combine_kernel, tm=tm),
        grid_spec=pltpu.PrefetchScalarGridSpec(
            num_scalar_prefetch=1,
            grid=(n // tm,),
            in_specs=[
                pl.BlockSpec((tm, D_MODEL), lambda i, d: (i, 0)),
                pl.BlockSpec((tm, 2), lambda i, d: (i, 0)),
                pl.BlockSpec((tm, PLE_DIM), lambda i, d: (i, 0)),
                pl.BlockSpec(memory_space=pl.ANY),
                const((PLE_DIM, D_MODEL)),
                const((1, D_MODEL)),
                const((D_MODEL, D_MODEL)),
                const((1, D_MODEL)),
            ],
            out_specs=pl.BlockSpec((tm, D_MODEL), lambda i, d: (i, 0)),
            scratch_shapes=[pltpu.VMEM((2, 2, tm * ROW_TILE, LANES), U32), pltpu.SemaphoreType.DMA((2,))],
        ),
        out_shape=jax.ShapeDtypeStruct((n, D_MODEL), F32),
        compiler_params=_cparams("arbitrary"),
        name="combine_ple",
    )(dest_flat, h1, wts, p2, y, w_ple_in, g_ple, w_ple_gate, g_final)


def _tile(n, pref):
    t = min(n, pref)
    assert n % t == 0, (n, t)
    return t


def kernel(x, p, g_mix, w_in, b_qkv, sinks, rel_bias, lb_logits, g_hgrn, w_up_att, w_up_rec, w_out, g_ffn,
           w_coarse, b_coarse, w_fine, b_fine, w_gate, w_up, w_down, w_ple_in, g_ple, w_ple_gate, g_final):
    B, T, D = x.shape
    assert D == D_MODEL and T % ATT_BLOCK == 0 and lb_logits.shape[0] == 2
    n = B * T
    layer = 0
    x2 = x.reshape(n, D)
    row = lambda a: a.reshape(1, -1)

    w_in_b = w_in[layer].astype(BF16)
    qkv = _qkv_proj(x2, row(g_mix[layer]), w_in_b[:, :ATT_QKV_W], row(b_qkv[layer]), _tile(n, 512))
    hz, f_raw = _hz_proj(x2, row(g_mix[layer]), w_in_b[:, ATT_QKV_W:], _tile(n, 1024))

    o_att = _swa(qkv, _bias_table(rel_bias), row(sinks[layer]), B, T)
    o_rec = _hgrn(hz, f_raw, lb_logits, row(g_hgrn[layer]), B, T, _tile(T, 512))

    w_router = jnp.concatenate(
        [w_coarse[layer], w_fine[layer], jnp.zeros((D, ROUTER_W - MOE_GROUPS - MOE_EXPERTS), F32)], axis=1)
    b_router = jnp.concatenate(
        [b_coarse[layer], b_fine[layer], jnp.zeros((ROUTER_W - MOE_GROUPS - MOE_EXPERTS,), F32)]).reshape(1, -1)
    h1, xn_packed, logits = _mix_out(
        o_att, o_rec, hz, x2, w_up_att[layer].astype(BF16), w_up_rec[layer].astype(BF16),
        w_out[layer].astype(BF16), row(g_ffn[layer]), w_router, b_router, _tile(n, 256))

    cap = 2 * n + MOE_EXPERTS * EXPERT_ROWS
    nblk = cap // EXPERT_ROWS
    nblk_pad = (nblk + 7) // 8 * 8
    dest, wts, blk_tab = _route(logits, _tile(n, 512), nblk_pad)
    dest_flat = dest.reshape(2 * n)
    blk_e = blk_tab[:nblk, 0]
    n_used = blk_tab[0, 1].reshape(1)

    xs = _dispatch(dest_flat, xn_packed, cap, _tile(n, 512))
    y = _experts(blk_e, n_used, xs, w_gate[layer].astype(BF16), w_up[layer].astype(BF16),
                 w_down[layer].astype(BF16))
    out = _combine_ple(dest_flat, h1, wts, p[layer].reshape(n, PLE_DIM), y, w_ple_in[layer].astype(BF16),
                       row(g_ple[layer]), w_ple_gate[layer].astype(BF16), row(g_final), _tile(n, 256))
    return out.reshape(B, T, D)
```

```python
import functools
import math

import numpy as np
import jax
import jax.numpy as jnp
from jax import lax
from jax.experimental import pallas as pl
from jax.experimental.pallas import tpu as pltpu

F32 = jnp.float32
BF16 = jnp.bfloat16
I32 = jnp.int32
U32 = jnp.uint32

D_MODEL = 2048
ATT_Q_HEADS = 16
ATT_KV_HEADS = 2
ATT_GROUP = ATT_Q_HEADS // ATT_KV_HEADS
ATT_HEAD_DIM = 64
WINDOW = 128
ATT_BLOCK = 128
ATT_Q_W = ATT_Q_HEADS * ATT_HEAD_DIM
ATT_KV_W = ATT_KV_HEADS * ATT_HEAD_DIM
ATT_QKV_W = ATT_Q_W + 2 * ATT_KV_W
REL_BUCKETS = 32
REL_MAX_DIST = 128
HGRN_HEADS = 8
HGRN_DK = 128
HGRN_DV = 128
HGRN_W = HGRN_HEADS * HGRN_DV
HGRN_CHUNK = 64
MOE_GROUPS = 8
MOE_EXPERTS_PER_GROUP = 8
MOE_EXPERTS = MOE_GROUPS * MOE_EXPERTS_PER_GROUP
MOE_HIDDEN = 512
PLE_DIM = 256
EPS = 1e-6

HALF = D_MODEL // 2
ROUTER_W = 128
EXPERT_ROWS = 256
VMEM_LIMIT = 56 * 1024 * 1024


def _cparams(*sem):
    return pltpu.CompilerParams(dimension_semantics=sem, vmem_limit_bytes=VMEM_LIMIT)


def _rms(xf):
    return xf * lax.rsqrt(jnp.mean(xf * xf, axis=-1, keepdims=True) + EPS)


def _sigmoid(x):
    return 0.5 + 0.5 * jnp.tanh(0.5 * x)


def _silu(x):
    h = 0.5 * x
    return h + h * jnp.tanh(h)


def _pack_bf16_pair(lo_f32, hi_f32):
    lo = lax.bitcast_convert_type(lo_f32.astype(BF16).astype(F32), U32)
    hi = lax.bitcast_convert_type(hi_f32.astype(BF16).astype(F32), U32)
    return hi | (lo >> 16)


def _unpack_bf16_pair(w):
    lo = lax.bitcast_convert_type(w << 16, F32)
    hi = lax.bitcast_convert_type(w & jnp.uint32(0xFFFF0000), F32)
    return lo, hi


LANES = 128
ROW_TILE = HALF // LANES


def _store_token_rows(ref, words):
    tokens = words.shape[0]
    for c in range(ROW_TILE):
        ref[pl.ds(c, tokens, stride=ROW_TILE), :] = words[:, c * LANES:(c + 1) * LANES]


def _load_token_rows(ref, tokens):
    return jnp.concatenate([ref[pl.ds(c, tokens, stride=ROW_TILE), :] for c in range(ROW_TILE)], axis=-1)


def _token_row(ref, idx):
    return ref.at[pl.ds(pl.multiple_of(idx * ROW_TILE, ROW_TILE), ROW_TILE), :]


def _qkv_kernel(x_ref, g_ref, w_ref, b_ref, o_ref):
    u = (_rms(x_ref[...]) * g_ref[...]).astype(BF16)
    acc = jnp.dot(u, w_ref[...], preferred_element_type=F32)
    o_ref[...] = (acc + b_ref[...]).astype(o_ref.dtype)


def _qkv_proj(x2, g_mix, w_qkv, b_qkv, tm):
    n = x2.shape[0]
    return pl.pallas_call(
        _qkv_kernel,
        grid=(n // tm,),
        in_specs=[
            pl.BlockSpec((tm, D_MODEL), lambda i: (i, 0)),
            pl.BlockSpec((1, D_MODEL), lambda i: (0, 0)),
            pl.BlockSpec((D_MODEL, ATT_QKV_W), lambda i: (0, 0)),
            pl.BlockSpec((1, ATT_QKV_W), lambda i: (0, 0)),
        ],
        out_specs=pl.BlockSpec((tm, ATT_QKV_W), lambda i: (i, 0)),
        out_shape=jax.ShapeDtypeStruct((n, ATT_QKV_W), BF16),
        compiler_params=_cparams("parallel"),
        name="qkv_proj",
    )(x2, g_mix, w_qkv, b_qkv)


HZ_TN = 1024
HZ_COLS = 4 * HGRN_W + 2 * D_MODEL
HZ_F_BLOCK = 1


def _hz_kernel(x_ref, g_ref, w_ref, hz_ref, f_ref, u_scr):
    j = pl.program_id(1)

    @pl.when(j == 0)
    def _():
        u_scr[...] = (_rms(x_ref[...]) * g_ref[...]).astype(BF16)

    acc = jnp.dot(u_scr[...], w_ref[...], preferred_element_type=F32)
    hz_ref[...] = acc.astype(BF16)

    @pl.when(j == HZ_F_BLOCK)
    def _():
        f_ref[...] = acc


def _hz_proj(x2, g_mix, w_hz, tm):
    n = x2.shape[0]
    return pl.pallas_call(
        _hz_kernel,
        grid=(n // tm, HZ_COLS // HZ_TN),
        in_specs=[
            pl.BlockSpec((tm, D_MODEL), lambda i, j: (i, 0)),
            pl.BlockSpec((1, D_MODEL), lambda i, j: (0, 0)),
            pl.BlockSpec((D_MODEL, HZ_TN), lambda i, j: (0, j)),
        ],
        out_specs=[
            pl.BlockSpec((tm, HZ_TN), lambda i, j: (i, j)),
            pl.BlockSpec((tm, HGRN_W), lambda i, j: (i, 0)),
        ],
        out_shape=[
            jax.ShapeDtypeStruct((n, HZ_COLS), BF16),
            jax.ShapeDtypeStruct((n, HGRN_W), F32),
        ],
        scratch_shapes=[pltpu.VMEM((tm, D_MODEL), BF16)],
        compiler_params=_cparams("parallel", "arbitrary"),
        name="hz_proj",
    )(x2, g_mix, w_hz)


def _t5_bucket_table():
    qi = np.arange(ATT_BLOCK)[:, None]
    kj = np.arange(2 * ATT_BLOCK)[None, :]
    dist = qi + ATT_BLOCK - kj
    exact = REL_BUCKETS // 2
    d = np.maximum(dist, 0)
    large = exact + (np.log(np.maximum(d, 1).astype(np.float32) / exact)
                     / math.log(REL_MAX_DIST / exact) * (REL_BUCKETS - exact)).astype(np.int32)
    large = np.minimum(large, REL_BUCKETS - 1)
    return np.where(d < exact, d, large).astype(np.int32)


def _bias_kernel(bucket_ref, rel_ref, o_ref):
    bk = bucket_ref[...]
    qi = lax.broadcasted_iota(I32, bk.shape, 0)
    kj = lax.broadcasted_iota(I32, bk.shape, 1)
    dist = qi + ATT_BLOCK - kj
    in_window = (dist >= 0) & (dist < WINDOW)
    first_block = in_window & (kj >= ATT_BLOCK)
    neg = jnp.float32(-jnp.inf)
    for h in range(ATT_Q_HEADS):
        acc = jnp.zeros(bk.shape, F32)
        for b in range(REL_BUCKETS):
            acc = jnp.where(bk == b, rel_ref[b, h], acc)
        o_ref[0, h] = jnp.where(first_block, acc, neg)
        o_ref[1, h] = jnp.where(in_window, acc, neg)


def _bias_table(rel_bias):
    bucket = jnp.asarray(_t5_bucket_table())
    return pl.pallas_call(
        _bias_kernel,
        in_specs=[
            pl.BlockSpec(memory_space=pltpu.VMEM),
            pl.BlockSpec(memory_space=pltpu.SMEM),
        ],
        out_specs=pl.BlockSpec(memory_space=pltpu.VMEM),
        out_shape=jax.ShapeDtypeStruct((2, ATT_Q_HEADS, ATT_BLOCK, 2 * ATT_BLOCK), F32),
        name="bias_table",
    )(bucket, rel_bias)


def _swa_kernel(q_ref, kvc_ref, kvp_ref, bias_ref, sink_ref, o_ref):
    hd = ATT_HEAD_DIM
    scale = hd ** -0.5
    for kk in range(ATT_KV_HEADS):
        k = jnp.concatenate([kvp_ref[:, hd * kk:hd * (kk + 1)], kvc_ref[:, hd * kk:hd * (kk + 1)]], axis=0)
        v = jnp.concatenate([kvp_ref[:, ATT_KV_W + hd * kk:ATT_KV_W + hd * (kk + 1)],
                             kvc_ref[:, ATT_KV_W + hd * kk:ATT_KV_W + hd * (kk + 1)]], axis=0)
        heads = range(kk * ATT_GROUP, (kk + 1) * ATT_GROUP)
        scores = [lax.dot_general(q_ref[:, hd * h:hd * (h + 1)], k, (((1,), (1,)), ((), ())),
                                  preferred_element_type=F32) for h in heads]
        probs, denoms = [], []
        for h, s in zip(heads, scores):
            s = s * scale + bias_ref[0, h]
            sink = sink_ref[0, h]
            m = jnp.maximum(jnp.max(s, axis=-1, keepdims=True), sink)
            p = jnp.exp(s - m)
            denoms.append(jnp.sum(p, axis=-1, keepdims=True) + jnp.exp(sink - m))
            probs.append(p.astype(BF16))
        outs = [jnp.dot(p, v, preferred_element_type=F32) for p in probs]
        for h, o, denom in zip(heads, outs, denoms):
            o_ref[:, hd * h:hd * (h + 1)] = (o / denom).astype(o_ref.dtype)


def _swa(qkv, bias_tab, sinks, batch, seq):
    n = qkv.shape[0]
    nb = seq // ATT_BLOCK
    kv_col = ATT_Q_W // (2 * ATT_KV_W)

    return pl.pallas_call(
        _swa_kernel,
        grid=(batch, nb),
        in_specs=[
            pl.BlockSpec((ATT_BLOCK, ATT_Q_W), lambda b, i: (b * nb + i, 0)),
            pl.BlockSpec((ATT_BLOCK, 2 * ATT_KV_W), lambda b, i: (b * nb + i, kv_col)),
            pl.BlockSpec((ATT_BLOCK, 2 * ATT_KV_W), lambda b, i: (b * nb + jnp.maximum(i - 1, 0), kv_col)),
            pl.BlockSpec((1, ATT_Q_HEADS, ATT_BLOCK, 2 * ATT_BLOCK), lambda b, i: (jnp.minimum(i, 1), 0, 0, 0)),
            pl.BlockSpec(memory_space=pltpu.SMEM),
        ],
        out_specs=pl.BlockSpec((ATT_BLOCK, ATT_Q_W), lambda b, i: (b * nb + i, 0)),
        out_shape=jax.ShapeDtypeStruct((n, ATT_Q_W), BF16),
        compiler_params=_cparams("parallel", "arbitrary"),
        name="swa",
    )(qkv, qkv, qkv, bias_tab, sinks)


def _cumsum_rows(tri_bf16, g):
    w = g.shape[1]
    g1 = g.astype(BF16)
    r1 = g - g1.astype(F32)
    g2 = r1.astype(BF16)
    g3 = (r1 - g2.astype(F32)).astype(BF16)
    parts = jnp.dot(tri_bf16, jnp.concatenate([g1, g2, g3], axis=1), preferred_element_type=F32)
    return parts[:, :w] + (parts[:, w:2 * w] + parts[:, 2 * w:])


def _hgrn_kernel(q_ref, f_ref, i_ref, g_ref, lbl_ref, gh_ref, o_ref, st_ref, *, chunks):
    C = HGRN_CHUNK
    ref_row = C // 2 - 1

    @pl.when(pl.program_id(1) == 0)
    def _():
        st_ref[...] = jnp.zeros_like(st_ref)

    l = lbl_ref[...]
    e = jnp.exp(l - jnp.max(l, axis=0, keepdims=True))
    lb_all = e[0:1, :] / jnp.sum(e, axis=0, keepdims=True)

    row = lax.broadcasted_iota(I32, (C, C), 0)
    col = lax.broadcasted_iota(I32, (C, C), 1)
    causal = row >= col
    tri = causal.astype(F32).astype(BF16)
    gain = gh_ref[...]

    def chunk(c, carry):
        r0 = pl.multiple_of(c * C, C)
        rows = pl.ds(r0, C)
        f = lb_all + (1.0 - lb_all) * _sigmoid(f_ref[rows, :])
        k = 1.0 - f
        b = _cumsum_rows(tri, jnp.log(f))
        qs = _silu(q_ref[rows, :].astype(F32))
        b_ref = b[ref_row:ref_row + 1, :]
        b_last = b[C - 1:C, :]
        qe_all = (qs * jnp.exp(b)).astype(BF16)
        qa_all = (qs * jnp.exp(b - b_ref)).astype(BF16)
        ka_all = (k * jnp.exp(b_ref - b)).astype(BF16)
        kl_all = (k * jnp.exp(b_last - b)).astype(BF16)
        decay = jnp.exp(b_last)
        gate = _silu(g_ref[rows, :].astype(F32))
        nt = (((1,), (1,)), ((), ()))
        heads = [slice(HGRN_DK * h, HGRN_DK * (h + 1)) for h in range(HGRN_HEADS)]
        vs = [i_ref[rows, cs] for cs in heads]
        sts = [st_ref[h] for h in range(HGRN_HEADS)]
        attn = [lax.dot_general(qa_all[:, cs], ka_all[:, cs], nt, preferred_element_type=F32) for cs in heads]
        inter = [lax.dot_general(qe_all[:, cs], st.astype(BF16), nt, preferred_element_type=F32)
                 for cs, st in zip(heads, sts)]
        upd = [lax.dot_general(v, kl_all[:, cs], (((0,), (0,)), ((), ())), preferred_element_type=F32)
               for cs, v in zip(heads, vs)]
        for h, cs in enumerate(heads):
            st_ref[h] = sts[h] * decay[:, cs] + upd[h]
        intra = [jnp.dot(jnp.where(causal, a, 0.0).astype(BF16), v, preferred_element_type=F32)
                 for a, v in zip(attn, vs)]
        for h, cs in enumerate(heads):
            o = inter[h] + intra[h]
            o_ref[rows, cs] = (_rms(o) * gain * gate[:, cs]).astype(o_ref.dtype)
        return carry

    lax.fori_loop(0, chunks, chunk, 0, unroll=2)


def _hgrn(hz, f_raw, lb_logits, g_hgrn, batch, seq, tc):
    n = hz.shape[0]
    nt = seq // tc
    blk = lambda col: pl.BlockSpec((tc, HGRN_W), lambda b, t, col=col: (b * nt + t, col))
    return pl.pallas_call(
        functools.partial(_hgrn_kernel, chunks=tc // HGRN_CHUNK),
        grid=(batch, nt),
        in_specs=[
            blk(0),
            pl.BlockSpec((tc, HGRN_W), lambda b, t: (b * nt + t, 0)),
            blk(2),
            blk(3),
            pl.BlockSpec((2, HGRN_W), lambda b, t: (0, 0)),
            pl.BlockSpec((1, HGRN_DV), lambda b, t: (0, 0)),
        ],
        out_specs=pl.BlockSpec((tc, HGRN_W), lambda b, t: (b * nt + t, 0)),
        out_shape=jax.ShapeDtypeStruct((n, HGRN_W), BF16),
        scratch_shapes=[pltpu.VMEM((HGRN_HEADS, HGRN_DV, HGRN_DK), F32)],
        compiler_params=_cparams("parallel", "arbitrary"),
        name="hgrn",
    )(hz, f_raw, hz, hz, lb_logits, g_hgrn)


def _split_hi_lo(a):
    hi = a.astype(BF16)
    lo = (a - hi.astype(F32)).astype(BF16)
    return hi, lo


def _mix_kernel(oa_ref, or_ref, ga_ref, gr_ref, x_ref, wa_ref, wr_ref, wo_ref, gf_ref, wrt_ref, brt_ref,
                h_ref, xn_ref, lg_ref):
    dot = functools.partial(jnp.dot, preferred_element_type=F32)
    ya = dot(oa_ref[...], wa_ref[...])
    yr = dot(or_ref[...], wr_ref[...])
    mix = _sigmoid(ga_ref[...].astype(F32)) * ya + _sigmoid(gr_ref[...].astype(F32)) * yr
    h = x_ref[...] + dot(mix.astype(BF16), wo_ref[...])
    h_ref[...] = h
    xn = _rms(h) * gf_ref[...]
    _store_token_rows(xn_ref, _pack_bf16_pair(xn[:, :HALF], xn[:, HALF:]))
    x_hi, x_lo = _split_hi_lo(xn)
    w_hi, w_lo = _split_hi_lo(wrt_ref[...])
    lg_ref[...] = dot(x_hi, w_hi) + (dot(x_hi, w_lo) + dot(x_lo, w_hi)) + brt_ref[...]


def _mix_out(o_att, o_rec, hz, x2, w_up_att, w_up_rec, w_out, g_ffn, w_router, b_router, tm):
    n = x2.shape[0]
    const = lambda shape: pl.BlockSpec(shape, lambda i: (0,) * len(shape))
    ga_col = 4 * HGRN_W // D_MODEL
    return pl.pallas_call(
        _mix_kernel,
        grid=(n // tm,),
        in_specs=[
            pl.BlockSpec((tm, ATT_Q_W), lambda i: (i, 0)),
            pl.BlockSpec((tm, HGRN_W), lambda i: (i, 0)),
            pl.BlockSpec((tm, D_MODEL), lambda i: (i, ga_col)),
            pl.BlockSpec((tm, D_MODEL), lambda i: (i, ga_col + 1)),
            pl.BlockSpec((tm, D_MODEL), lambda i: (i, 0)),
            const((ATT_Q_W, D_MODEL)),
            const((HGRN_W, D_MODEL)),
            const((D_MODEL, D_MODEL)),
            const((1, D_MODEL)),
            const((D_MODEL, ROUTER_W)),
            const((1, ROUTER_W)),
        ],
        out_specs=[
            pl.BlockSpec((tm, D_MODEL), lambda i: (i, 0)),
            pl.BlockSpec((tm * ROW_TILE, LANES), lambda i: (i, 0)),
            pl.BlockSpec((tm, ROUTER_W), lambda i: (i, 0)),
        ],
        out_shape=[
            jax.ShapeDtypeStruct((n, D_MODEL), F32),
            jax.ShapeDtypeStruct((n * ROW_TILE, LANES), U32),
            jax.ShapeDtypeStruct((n, ROUTER_W), F32),
        ],
        compiler_params=_cparams("parallel"),
        name="mix_out",
    )(o_att, o_rec, hz, hz, x2, w_up_att, w_up_rec, w_out, g_ffn, w_router, b_router)


def _route_choice(lg):
    tm = lg.shape[0]
    lane = lax.broadcasted_iota(I32, lg.shape, 1)
    neg = jnp.float32(-jnp.inf)
    is_coarse = lane < MOE_GROUPS
    lc = jnp.where(is_coarse, lg, neg)
    cmax = jnp.max(lc, axis=-1, keepdims=True)
    p_sel = 1.0 / jnp.sum(jnp.exp(lc - cmax), axis=-1, keepdims=True)
    grp = jnp.min(jnp.where(lc == cmax, lane, ROUTER_W), axis=-1, keepdims=True)
    fine_lane = lane - MOE_GROUPS
    in_grp = (fine_lane >= grp * MOE_EXPERTS_PER_GROUP) & (fine_lane < (grp + 1) * MOE_EXPERTS_PER_GROUP)
    lf = jnp.where(in_grp, lg, neg)
    v0 = jnp.max(lf, axis=-1, keepdims=True)
    e0 = jnp.min(jnp.where(lf == v0, fine_lane, ROUTER_W), axis=-1, keepdims=True)
    lf1 = jnp.where(fine_lane == e0, neg, lf)
    v1 = jnp.max(lf1, axis=-1, keepdims=True)
    e1 = jnp.min(jnp.where(lf1 == v1, fine_lane, ROUTER_W), axis=-1, keepdims=True)
    t1 = jnp.exp(v1 - v0)
    w0 = p_sel / (1.0 + t1)
    w1 = p_sel * t1 / (1.0 + t1)
    return e0, e1, w0, w1


def _route_kernel(lg_ref, dest_ref, wts_ref, blk_ref, cnt_ref, *, tm, nblk_pad):
    phase = pl.program_id(0)
    i = pl.program_id(1)
    E = MOE_EXPERTS

    @pl.when((phase == 0) & (i == 0))
    def _():
        cnt_ref[...] = jnp.zeros_like(cnt_ref)

    e0, e1, w0, w1 = _route_choice(lg_ref[...])
    elane = lax.broadcasted_iota(I32, (tm, E), 1)
    oh0 = elane == e0
    oh1 = elane == e1
    picks = (oh0 | oh1).astype(F32)

    @pl.when(phase == 0)
    def _():
        cnt_ref[0:1, :] = cnt_ref[0:1, :] + jnp.sum(picks, axis=0, keepdims=True)

    @pl.when(phase == 1)
    def _():
        dot = functools.partial(jnp.dot, preferred_element_type=F32)
        nb = jnp.floor((cnt_ref[0:1, :] + (EXPERT_ROWS - 1)) * (1.0 / EXPERT_ROWS))
        nb_hi = jnp.floor(nb * (1.0 / 32.0))
        nb_lo = nb - 32.0 * nb_hi
        r = lax.broadcasted_iota(I32, (E, E), 0)
        c = lax.broadcasted_iota(I32, (E, E), 1)
        upper = (r < c).astype(F32).astype(BF16)
        incl = (r <= c).astype(F32).astype(BF16)
        nb8_hi = jnp.broadcast_to(nb_hi, (8, E)).astype(BF16)
        nb8_lo = jnp.broadcast_to(nb_lo, (8, E)).astype(BF16)
        start_blk = 32.0 * dot(nb8_hi, upper) + dot(nb8_lo, upper)
        end_blk = 32.0 * dot(nb8_hi, incl) + dot(nb8_lo, incl)

        @pl.when(i == 0)
        def _():
            cnt_ref[1:2, :] = jnp.zeros((1, E), F32)
            bidx = lax.broadcasted_iota(I32, (nblk_pad, E), 0).astype(F32)
            be = jnp.sum((end_blk[0:1, :] <= bidx).astype(F32), axis=-1, keepdims=True)
            be = jnp.minimum(be, float(E - 1))
            lane = lax.broadcasted_iota(I32, blk_ref.shape, 1)
            blk_ref[...] = jnp.where(lane == 1, end_blk[0:1, E - 1:E], be).astype(I32)

        tr = lax.broadcasted_iota(I32, (tm, tm), 0)
        tc = lax.broadcasted_iota(I32, (tm, tm), 1)
        strict = (tc < tr).astype(F32).astype(BF16)
        before = dot(strict, picks.astype(BF16))
        pos = start_blk[0:1, :] * float(EXPERT_ROWS) + cnt_ref[1:2, :] + before
        d0 = jnp.sum(jnp.where(oh0, pos, 0.0), axis=-1, keepdims=True)
        d1 = jnp.sum(jnp.where(oh1, pos, 0.0), axis=-1, keepdims=True)
        cnt_ref[1:2, :] = cnt_ref[1:2, :] + jnp.sum(picks, axis=0, keepdims=True)
        two = lax.broadcasted_iota(I32, (tm, 2), 1)
        dest_ref[...] = jnp.where(two == 0, d0, d1).astype(I32)
        wts_ref[...] = jnp.where(two == 0, w0, w1)


def _route(logits, tm, nblk_pad):
    n = logits.shape[0]
    return pl.pallas_call(
        functools.partial(_route_kernel, tm=tm, nblk_pad=nblk_pad),
        grid=(2, n // tm),
        in_specs=[pl.BlockSpec((tm, ROUTER_W), lambda p, i: (i, 0))],
        out_specs=[
            pl.BlockSpec((tm, 2), lambda p, i: (i * p, 0)),
            pl.BlockSpec((tm, 2), lambda p, i: (i * p, 0)),
            pl.BlockSpec((nblk_pad, 128), lambda p, i: (0, 0)),
        ],
        out_shape=[
            jax.ShapeDtypeStruct((n, 2), I32),
            jax.ShapeDtypeStruct((n, 2), F32),
            jax.ShapeDtypeStruct((nblk_pad, 128), I32),
        ],
        scratch_shapes=[pltpu.VMEM((8, MOE_EXPERTS), F32)],
        compiler_params=_cparams("arbitrary", "arbitrary"),
        name="route",
    )(logits)


ISSUE_UNROLL = 8


def _dispatch_kernel(dest_ref, blk_ref, nused_ref, xn_ref, xs_ref, zbuf, sem, zsem, *, tm, nblk):
    i = pl.program_id(0)
    base = i * tm
    block_rows = EXPERT_ROWS * ROW_TILE

    @pl.when(i == 0)
    def _():
        zbuf[...] = jnp.zeros_like(zbuf)
        last_used = nused_ref[0] - 1

        def needs_clear(j):
            return (j >= last_used) | (blk_ref[j] != blk_ref[jnp.minimum(j + 1, nblk - 1)])

        def clear(j):
            rows = pl.ds(pl.multiple_of(j * block_rows, block_rows), block_rows)
            return pltpu.make_async_copy(zbuf, xs_ref.at[rows, :], zsem)

        def start(j, c):
            @pl.when(needs_clear(j))
            def _():
                clear(j).start()
            return c

        def wait(j, c):
            @pl.when(needs_clear(j))
            def _():
                clear(j).wait()
            return c

        lax.fori_loop(0, nblk, start, 0)
        lax.fori_loop(0, nblk, wait, 0)

    def issue(t, c):
        for k in range(2):
            pltpu.make_async_copy(_token_row(xn_ref, t),
                                  _token_row(xs_ref, dest_ref[2 * (base + t) + k]), sem).start()
        return c

    lax.fori_loop(0, tm, issue, 0, unroll=ISSUE_UNROLL)
    for k in range(2):
        pltpu.make_async_copy(xn_ref, xs_ref.at[pl.ds(0, tm * ROW_TILE), :], sem).wait()


def _dispatch(dest_flat, blk_e, n_used, xn_packed, cap, tm):
    n = xn_packed.shape[0] // ROW_TILE
    nblk = cap // EXPERT_ROWS
    return pl.pallas_call(
        functools.partial(_dispatch_kernel, tm=tm, nblk=nblk),
        grid_spec=pltpu.PrefetchScalarGridSpec(
            num_scalar_prefetch=3,
            grid=(n // tm,),
            in_specs=[pl.BlockSpec((tm * ROW_TILE, LANES), lambda i, d, b, u: (i, 0))],
            out_specs=pl.BlockSpec(memory_space=pl.ANY),
            scratch_shapes=[pltpu.VMEM((EXPERT_ROWS * ROW_TILE, LANES), U32),
                            pltpu.SemaphoreType.DMA, pltpu.SemaphoreType.DMA],
        ),
        out_shape=jax.ShapeDtypeStruct((cap * ROW_TILE, LANES), U32),
        compiler_params=_cparams("arbitrary"),
        name="dispatch",
    )(dest_flat, blk_e, n_used, xn_packed)


def _expert_kernel(blk_ref, nused_ref, xs_ref, wg_ref, wu_ref, wd_ref, y_ref, wg_b, wu_b, wd_b):
    i = pl.program_id(0)
    active = i < nused_ref[0]
    new_expert = (i == 0) | (blk_ref[i] != blk_ref[jnp.maximum(i - 1, 0)])

    @pl.when(jnp.logical_not(active))
    def _():
        y_ref[...] = jnp.zeros_like(y_ref)

    @pl.when(active & new_expert)
    def _():
        wg_b[...] = wg_ref[0].astype(BF16)
        wu_b[...] = wu_ref[0].astype(BF16)
        wd_b[...] = wd_ref[0].astype(BF16)

    @pl.when(active)
    def _():
        dot = functools.partial(jnp.dot, preferred_element_type=F32)
        lo, hi = _unpack_bf16_pair(_load_token_rows(xs_ref, EXPERT_ROWS))
        lo = lo.astype(BF16)
        hi = hi.astype(BF16)
        g = dot(lo, wg_b[:HALF, :]) + dot(hi, wg_b[HALF:, :])
        u = dot(lo, wu_b[:HALF, :]) + dot(hi, wu_b[HALF:, :])
        hid = (_silu(g) * u).astype(BF16)
        y = dot(hid, wd_b[...])
        _store_token_rows(y_ref, _pack_bf16_pair(y[:, :HALF], y[:, HALF:]))


def _experts(blk_e, n_used, xs, w_gate, w_up, w_down):
    cap = xs.shape[0] // ROW_TILE
    nblk = cap // EXPERT_ROWS
    row = lambda i, blk, nu: (jnp.minimum(i, nu[0] - 1), 0)
    wsel = lambda i, blk, nu: (blk[jnp.minimum(i, nu[0] - 1)], 0, 0)
    return pl.pallas_call(
        _expert_kernel,
        grid_spec=pltpu.PrefetchScalarGridSpec(
            num_scalar_prefetch=2,
            grid=(nblk,),
            in_specs=[
                pl.BlockSpec((EXPERT_ROWS * ROW_TILE, LANES), row),
                pl.BlockSpec((1, D_MODEL, MOE_HIDDEN), wsel),
                pl.BlockSpec((1, D_MODEL, MOE_HIDDEN), wsel),
                pl.BlockSpec((1, MOE_HIDDEN, D_MODEL), wsel),
            ],
            out_specs=pl.BlockSpec((EXPERT_ROWS * ROW_TILE, LANES), lambda i, blk, nu: (i, 0)),
            scratch_shapes=[pltpu.VMEM((D_MODEL, MOE_HIDDEN), BF16), pltpu.VMEM((D_MODEL, MOE_HIDDEN), BF16),
                            pltpu.VMEM((MOE_HIDDEN, D_MODEL), BF16)],
        ),
        out_shape=jax.ShapeDtypeStruct((cap * ROW_TILE, LANES), U32),
        compiler_params=_cparams("arbitrary"),
        name="experts",
    )(blk_e, n_used, xs, w_gate, w_up, w_down)


def _combine_kernel(dest_ref, h_ref, wts_ref, p_ref, y_ref, wpi_ref, gp_ref, wpg_ref, gfin_ref, o_ref,
                    ybuf, sems, *, tm):
    i = pl.program_id(0)
    slot = lax.rem(i, 2)

    def gather(tile, slot):
        base = tile * tm

        def issue(t, c):
            for k in range(2):
                pltpu.make_async_copy(_token_row(y_ref, dest_ref[2 * (base + t) + k]),
                                      _token_row(ybuf.at[slot, k], t), sems.at[slot]).start()
            return c

        lax.fori_loop(0, tm, issue, 0, unroll=ISSUE_UNROLL)

    @pl.when(i == 0)
    def _():
        gather(0, 0)

    @pl.when(i + 1 < pl.num_programs(0))
    def _():
        gather(i + 1, 1 - slot)

    dot = functools.partial(jnp.dot, preferred_element_type=F32)
    ple = _rms(dot(p_ref[...].astype(BF16), wpi_ref[...])) * gp_ref[...]

    for k in range(2):
        pltpu.make_async_copy(y_ref.at[pl.ds(0, tm * ROW_TILE), :], ybuf.at[slot, k], sems.at[slot]).wait()

    w = wts_ref[...]
    w0 = w[:, 0:1]
    w1 = w[:, 1:2]
    lo0, hi0 = _unpack_bf16_pair(_load_token_rows(ybuf.at[slot, 0], tm))
    lo1, hi1 = _unpack_bf16_pair(_load_token_rows(ybuf.at[slot, 1], tm))
    moe = jnp.concatenate([w0 * lo0 + w1 * lo1, w0 * hi0 + w1 * hi1], axis=-1)
    h = h_ref[...] + moe
    gate = _sigmoid(dot(_rms(h).astype(BF16), wpg_ref[...]))
    h = h + ple * gate
    o_ref[...] = _rms(h) * gfin_ref[...]


def _combine_ple(dest_flat, h1, wts, p2, y, w_ple_in, g_ple, w_ple_gate, g_final, tm):
    n = h1.shape[0]
    const = lambda shape: pl.BlockSpec(shape, lambda i, d: (0,) * len(shape))
    return pl.pallas_call(
        functools.partial(_combine_kernel, tm=tm),
        grid_spec=pltpu.PrefetchScalarGridSpec(
            num_scalar_prefetch=1,
            grid=(n // tm,),
            in_specs=[
                pl.BlockSpec((tm, D_MODEL), lambda i, d: (i, 0)),
                pl.BlockSpec((tm, 2), lambda i, d: (i, 0)),
                pl.BlockSpec((tm, PLE_DIM), lambda i, d: (i, 0)),
                pl.BlockSpec(memory_space=pl.ANY),
                const((PLE_DIM, D_MODEL)),
                const((1, D_MODEL)),
                const((D_MODEL, D_MODEL)),
                const((1, D_MODEL)),
            ],
            out_specs=pl.BlockSpec((tm, D_MODEL), lambda i, d: (i, 0)),
            scratch_shapes=[pltpu.VMEM((2, 2, tm * ROW_TILE, LANES), U32), pltpu.SemaphoreType.DMA((2,))],
        ),
        out_shape=jax.ShapeDtypeStruct((n, D_MODEL), F32),
        compiler_params=_cparams("arbitrary"),
        name="combine_ple",
    )(dest_flat, h1, wts, p2, y, w_ple_in, g_ple, w_ple_gate, g_final)


def _tile(n, pref):
    t = min(n, pref)
    assert n % t == 0, (n, t)
    return t


def kernel(x, p, g_mix, w_in, b_qkv, sinks, rel_bias, lb_logits, g_hgrn, w_up_att, w_up_rec, w_out, g_ffn,
           w_coarse, b_coarse, w_fine, b_fine, w_gate, w_up, w_down, w_ple_in, g_ple, w_ple_gate, g_final):
    B, T, D = x.shape
    assert D == D_MODEL and T % ATT_BLOCK == 0 and lb_logits.shape[0] == 2
    n = B * T
    layer = 0
    x2 = x.reshape(n, D)
    row = lambda a: a.reshape(1, -1)

    w_in_b = w_in[layer].astype(BF16)
    qkv = _qkv_proj(x2, row(g_mix[layer]), w_in_b[:, :ATT_QKV_W], row(b_qkv[layer]), _tile(n, 512))
    hz, f_raw = _hz_proj(x2, row(g_mix[layer]), w_in_b[:, ATT_QKV_W:], _tile(n, 1024))

    o_att = _swa(qkv, _bias_table(rel_bias), row(sinks[layer]), B, T)
    o_rec = _hgrn(hz, f_raw, lb_logits, row(g_hgrn[layer]), B, T, _tile(T, 512))

    w_router = jnp.concatenate(
        [w_coarse[layer], w_fine[layer], jnp.zeros((D, ROUTER_W - MOE_GROUPS - MOE_EXPERTS), F32)], axis=1)
    b_router = jnp.concatenate(
        [b_coarse[layer], b_fine[layer], jnp.zeros((ROUTER_W - MOE_GROUPS - MOE_EXPERTS,), F32)]).reshape(1, -1)
    h1, xn_packed, logits = _mix_out(
        o_att, o_rec, hz, x2, w_up_att[layer].astype(BF16), w_up_rec[layer].astype(BF16),
        w_out[layer].astype(BF16), row(g_ffn[layer]), w_router, b_router, _tile(n, 256))

    cap = 2 * n + MOE_EXPERTS * EXPERT_ROWS
    nblk = cap // EXPERT_ROWS
    nblk_pad = (nblk + 7) // 8 * 8
    dest, wts, blk_tab = _route(logits, _tile(n, 512), nblk_pad)
    dest_flat = dest.reshape(2 * n)
    blk_e = blk_tab[:nblk, 0]
    n_used = blk_tab[0, 1].reshape(1)

    xs = _dispatch(dest_flat, blk_e, n_used, xn_packed, cap, _tile(n, 512))
    y = _experts(blk_e, n_used, xs, w_gate[layer], w_up[layer], w_down[layer])
    out = _combine_ple(dest_flat, h1, wts, p[layer].reshape(n, PLE_DIM), y, w_ple_in[layer].astype(BF16),
                       row(g_ple[layer]), w_ple_gate[layer].astype(BF16), row(g_final), _tile(n, 256))
    return out.reshape(B, T, D)
```

```python
import functools
import math

import numpy as np
import jax
import jax.numpy as jnp
from jax import lax
from jax.experimental import pallas as pl
from jax.experimental.pallas import tpu as pltpu

F32 = jnp.float32
BF16 = jnp.bfloat16
I32 = jnp.int32
U32 = jnp.uint32

D_MODEL = 2048
ATT_Q_HEADS = 16
ATT_KV_HEADS = 2
ATT_GROUP = ATT_Q_HEADS // ATT_KV_HEADS
ATT_HEAD_DIM = 64
WINDOW = 128
ATT_BLOCK = 128
ATT_Q_W = ATT_Q_HEADS * ATT_HEAD_DIM
ATT_KV_W = ATT_KV_HEADS * ATT_HEAD_DIM
ATT_QKV_W = ATT_Q_W + 2 * ATT_KV_W
REL_BUCKETS = 32
REL_MAX_DIST = 128
HGRN_HEADS = 8
HGRN_DK = 128
HGRN_DV = 128
HGRN_W = HGRN_HEADS * HGRN_DV
HGRN_CHUNK = 64
MOE_GROUPS = 8
MOE_EXPERTS_PER_GROUP = 8
MOE_EXPERTS = MOE_GROUPS * MOE_EXPERTS_PER_GROUP
MOE_HIDDEN = 512
PLE_DIM = 256
EPS = 1e-6

HALF = D_MODEL // 2
ROUTER_W = 128
EXPERT_ROWS = 256
VMEM_LIMIT = 56 * 1024 * 1024


def _cparams(*sem):
    return pltpu.CompilerParams(dimension_semantics=sem, vmem_limit_bytes=VMEM_LIMIT)


def _rms(xf):
    return xf * lax.rsqrt(jnp.mean(xf * xf, axis=-1, keepdims=True) + EPS)


def _sigmoid(x):
    return 0.5 + 0.5 * jnp.tanh(0.5 * x)


def _silu(x):
    h = 0.5 * x
    return h + h * jnp.tanh(h)


def _pack_bf16_pair(lo_f32, hi_f32):
    lo = lax.bitcast_convert_type(lo_f32.astype(BF16).astype(F32), U32)
    hi = lax.bitcast_convert_type(hi_f32.astype(BF16).astype(F32), U32)
    return hi | (lo >> 16)


def _unpack_bf16_pair(w):
    lo = lax.bitcast_convert_type(w << 16, F32)
    hi = lax.bitcast_convert_type(w & jnp.uint32(0xFFFF0000), F32)
    return lo, hi


LANES = 128
ROW_TILE = HALF // LANES


def _store_token_rows(ref, words, first=0):
    tokens = words.shape[0]
    for c in range(ROW_TILE):
        ref[pl.ds(first * ROW_TILE + c, tokens, stride=ROW_TILE), :] = words[:, c * LANES:(c + 1) * LANES]


def _load_token_rows(ref, tokens, first=0):
    return jnp.concatenate([ref[pl.ds(first * ROW_TILE + c, tokens, stride=ROW_TILE), :]
                            for c in range(ROW_TILE)], axis=-1)


def _token_row(ref, idx):
    return ref.at[pl.ds(pl.multiple_of(idx * ROW_TILE, ROW_TILE), ROW_TILE), :]


def _qkv_kernel(x_ref, g_ref, w_ref, b_ref, o_ref):
    u = (_rms(x_ref[...]) * g_ref[...]).astype(BF16)
    acc = jnp.dot(u, w_ref[...], preferred_element_type=F32)
    o_ref[...] = (acc + b_ref[...]).astype(o_ref.dtype)


def _qkv_proj(x2, g_mix, w_qkv, b_qkv, tm):
    n = x2.shape[0]
    return pl.pallas_call(
        _qkv_kernel,
        grid=(n // tm,),
        in_specs=[
            pl.BlockSpec((tm, D_MODEL), lambda i: (i, 0)),
            pl.BlockSpec((1, D_MODEL), lambda i: (0, 0)),
            pl.BlockSpec((D_MODEL, ATT_QKV_W), lambda i: (0, 0)),
            pl.BlockSpec((1, ATT_QKV_W), lambda i: (0, 0)),
        ],
        out_specs=pl.BlockSpec((tm, ATT_QKV_W), lambda i: (i, 0)),
        out_shape=jax.ShapeDtypeStruct((n, ATT_QKV_W), BF16),
        compiler_params=_cparams("parallel"),
        name="qkv_proj",
    )(x2, g_mix, w_qkv, b_qkv)


HZ_TN = 1024
HZ_COLS = 4 * HGRN_W + 2 * D_MODEL
HZ_F_BLOCK = 1


def _hz_kernel(x_ref, g_ref, w_ref, hz_ref, f_ref, u_scr):
    j = pl.program_id(1)

    @pl.when(j == 0)
    def _():
        u_scr[...] = (_rms(x_ref[...]) * g_ref[...]).astype(BF16)

    acc = jnp.dot(u_scr[...], w_ref[...], preferred_element_type=F32)
    hz_ref[...] = acc.astype(BF16)

    @pl.when(j == HZ_F_BLOCK)
    def _():
        f_ref[...] = acc


def _hz_proj(x2, g_mix, w_hz, tm):
    n = x2.shape[0]
    return pl.pallas_call(
        _hz_kernel,
        grid=(n // tm, HZ_COLS // HZ_TN),
        in_specs=[
            pl.BlockSpec((tm, D_MODEL), lambda i, j: (i, 0)),
            pl.BlockSpec((1, D_MODEL), lambda i, j: (0, 0)),
            pl.BlockSpec((D_MODEL, HZ_TN), lambda i, j: (0, j)),
        ],
        out_specs=[
            pl.BlockSpec((tm, HZ_TN), lambda i, j: (i, j)),
            pl.BlockSpec((tm, HGRN_W), lambda i, j: (i, 0)),
        ],
        out_shape=[
            jax.ShapeDtypeStruct((n, HZ_COLS), BF16),
            jax.ShapeDtypeStruct((n, HGRN_W), F32),
        ],
        scratch_shapes=[pltpu.VMEM((tm, D_MODEL), BF16)],
        compiler_params=_cparams("parallel", "arbitrary"),
        name="hz_proj",
    )(x2, g_mix, w_hz)


def _t5_bucket_table():
    qi = np.arange(ATT_BLOCK)[:, None]
    kj = np.arange(2 * ATT_BLOCK)[None, :]
    dist = qi + ATT_BLOCK - kj
    exact = REL_BUCKETS // 2
    d = np.maximum(dist, 0)
    large = exact + (np.log(np.maximum(d, 1).astype(np.float32) / exact)
                     / math.log(REL_MAX_DIST / exact) * (REL_BUCKETS - exact)).astype(np.int32)
    large = np.minimum(large, REL_BUCKETS - 1)
    return np.where(d < exact, d, large).astype(np.int32)


def _bias_kernel(bucket_ref, rel_ref, o_ref):
    bk = bucket_ref[...]
    qi = lax.broadcasted_iota(I32, bk.shape, 0)
    kj = lax.broadcasted_iota(I32, bk.shape, 1)
    dist = qi + ATT_BLOCK - kj
    in_window = (dist >= 0) & (dist < WINDOW)
    first_block = in_window & (kj >= ATT_BLOCK)
    neg = jnp.float32(-jnp.inf)
    for h in range(ATT_Q_HEADS):
        acc = jnp.zeros(bk.shape, F32)
        for b in range(REL_BUCKETS):
            acc = jnp.where(bk == b, rel_ref[b, h], acc)
        o_ref[0, h] = jnp.where(first_block, acc, neg)
        o_ref[1, h] = jnp.where(in_window, acc, neg)


def _bias_table(rel_bias):
    bucket = jnp.asarray(_t5_bucket_table())
    return pl.pallas_call(
        _bias_kernel,
        in_specs=[
            pl.BlockSpec(memory_space=pltpu.VMEM),
            pl.BlockSpec(memory_space=pltpu.SMEM),
        ],
        out_specs=pl.BlockSpec(memory_space=pltpu.VMEM),
        out_shape=jax.ShapeDtypeStruct((2, ATT_Q_HEADS, ATT_BLOCK, 2 * ATT_BLOCK), F32),
        name="bias_table",
    )(bucket, rel_bias)


def _swa_kernel(q_ref, kvc_ref, kvp_ref, bias_ref, sink_ref, o_ref):
    hd = ATT_HEAD_DIM
    scale = hd ** -0.5
    for kk in range(ATT_KV_HEADS):
        k = jnp.concatenate([kvp_ref[:, hd * kk:hd * (kk + 1)], kvc_ref[:, hd * kk:hd * (kk + 1)]], axis=0)
        v = jnp.concatenate([kvp_ref[:, ATT_KV_W + hd * kk:ATT_KV_W + hd * (kk + 1)],
                             kvc_ref[:, ATT_KV_W + hd * kk:ATT_KV_W + hd * (kk + 1)]], axis=0)
        heads = range(kk * ATT_GROUP, (kk + 1) * ATT_GROUP)
        scores = [lax.dot_general(q_ref[:, hd * h:hd * (h + 1)], k, (((1,), (1,)), ((), ())),
                                  preferred_element_type=F32) for h in heads]
        probs, denoms = [], []
        for h, s in zip(heads, scores):
            s = s * scale + bias_ref[0, h]
            sink = sink_ref[0, h]
            m = jnp.maximum(jnp.max(s, axis=-1, keepdims=True), sink)
            p = jnp.exp(s - m)
            denoms.append(jnp.sum(p, axis=-1, keepdims=True) + jnp.exp(sink - m))
            probs.append(p.astype(BF16))
        outs = [jnp.dot(p, v, preferred_element_type=F32) for p in probs]
        for h, o, denom in zip(heads, outs, denoms):
            o_ref[:, hd * h:hd * (h + 1)] = (o / denom).astype(o_ref.dtype)


def _swa(qkv, bias_tab, sinks, batch, seq):
    n = qkv.shape[0]
    nb = seq // ATT_BLOCK
    kv_col = ATT_Q_W // (2 * ATT_KV_W)

    return pl.pallas_call(
        _swa_kernel,
        grid=(batch, nb),
        in_specs=[
            pl.BlockSpec((ATT_BLOCK, ATT_Q_W), lambda b, i: (b * nb + i, 0)),
            pl.BlockSpec((ATT_BLOCK, 2 * ATT_KV_W), lambda b, i: (b * nb + i, kv_col)),
            pl.BlockSpec((ATT_BLOCK, 2 * ATT_KV_W), lambda b, i: (b * nb + jnp.maximum(i - 1, 0), kv_col)),
            pl.BlockSpec((1, ATT_Q_HEADS, ATT_BLOCK, 2 * ATT_BLOCK), lambda b, i: (jnp.minimum(i, 1), 0, 0, 0)),
            pl.BlockSpec(memory_space=pltpu.SMEM),
        ],
        out_specs=pl.BlockSpec((ATT_BLOCK, ATT_Q_W), lambda b, i: (b * nb + i, 0)),
        out_shape=jax.ShapeDtypeStruct((n, ATT_Q_W), BF16),
        compiler_params=_cparams("parallel", "arbitrary"),
        name="swa",
    )(qkv, qkv, qkv, bias_tab, sinks)


def _cumsum_rows(tri_bf16, g):
    w = g.shape[1]
    g1 = g.astype(BF16)
    r1 = g - g1.astype(F32)
    g2 = r1.astype(BF16)
    g3 = (r1 - g2.astype(F32)).astype(BF16)
    parts = jnp.dot(tri_bf16, jnp.concatenate([g1, g2, g3], axis=1), preferred_element_type=F32)
    return parts[:, :w] + (parts[:, w:2 * w] + parts[:, 2 * w:])


def _hgrn_kernel(q_ref, f_ref, i_ref, g_ref, lbl_ref, gh_ref, o_ref, st_ref, *, chunks):
    C = HGRN_CHUNK
    ref_row = C // 2 - 1

    @pl.when(pl.program_id(1) == 0)
    def _():
        st_ref[...] = jnp.zeros_like(st_ref)

    l = lbl_ref[...]
    e = jnp.exp(l - jnp.max(l, axis=0, keepdims=True))
    lb_all = e[0:1, :] / jnp.sum(e, axis=0, keepdims=True)

    row = lax.broadcasted_iota(I32, (C, C), 0)
    col = lax.broadcasted_iota(I32, (C, C), 1)
    causal = row >= col
    tri = causal.astype(F32).astype(BF16)
    gain = gh_ref[...]

    def chunk(c, carry):
        r0 = pl.multiple_of(c * C, C)
        rows = pl.ds(r0, C)
        f = lb_all + (1.0 - lb_all) * _sigmoid(f_ref[rows, :])
        k = 1.0 - f
        b = _cumsum_rows(tri, jnp.log(f))
        qs = _silu(q_ref[rows, :].astype(F32))
        b_ref = b[ref_row:ref_row + 1, :]
        b_last = b[C - 1:C, :]
        qe_all = (qs * jnp.exp(b)).astype(BF16)
        qa_all = (qs * jnp.exp(b - b_ref)).astype(BF16)
        ka_all = (k * jnp.exp(b_ref - b)).astype(BF16)
        kl_all = (k * jnp.exp(b_last - b)).astype(BF16)
        decay = jnp.exp(b_last)
        gate = _silu(g_ref[rows, :].astype(F32))
        nt = (((1,), (1,)), ((), ()))
        heads = [slice(HGRN_DK * h, HGRN_DK * (h + 1)) for h in range(HGRN_HEADS)]
        vs = [i_ref[rows, cs] for cs in heads]
        sts = [st_ref[h] for h in range(HGRN_HEADS)]
        attn = [lax.dot_general(qa_all[:, cs], ka_all[:, cs], nt, preferred_element_type=F32) for cs in heads]
        inter = [lax.dot_general(qe_all[:, cs], st.astype(BF16), nt, preferred_element_type=F32)
                 for cs, st in zip(heads, sts)]
        upd = [lax.dot_general(v, kl_all[:, cs], (((0,), (0,)), ((), ())), preferred_element_type=F32)
               for cs, v in zip(heads, vs)]
        for h, cs in enumerate(heads):
            st_ref[h] = sts[h] * decay[:, cs] + upd[h]
        intra = [jnp.dot(jnp.where(causal, a, 0.0).astype(BF16), v, preferred_element_type=F32)
                 for a, v in zip(attn, vs)]
        for h, cs in enumerate(heads):
            o = inter[h] + intra[h]
            o_ref[rows, cs] = (_rms(o) * gain * gate[:, cs]).astype(o_ref.dtype)
        return carry

    lax.fori_loop(0, chunks, chunk, 0, unroll=2)


def _hgrn(hz, f_raw, lb_logits, g_hgrn, batch, seq, tc):
    n = hz.shape[0]
    nt = seq // tc
    blk = lambda col: pl.BlockSpec((tc, HGRN_W), lambda b, t, col=col: (b * nt + t, col))
    return pl.pallas_call(
        functools.partial(_hgrn_kernel, chunks=tc // HGRN_CHUNK),
        grid=(batch, nt),
        in_specs=[
            blk(0),
            pl.BlockSpec((tc, HGRN_W), lambda b, t: (b * nt + t, 0)),
            blk(2),
            blk(3),
            pl.BlockSpec((2, HGRN_W), lambda b, t: (0, 0)),
            pl.BlockSpec((1, HGRN_DV), lambda b, t: (0, 0)),
        ],
        out_specs=pl.BlockSpec((tc, HGRN_W), lambda b, t: (b * nt + t, 0)),
        out_shape=jax.ShapeDtypeStruct((n, HGRN_W), BF16),
        scratch_shapes=[pltpu.VMEM((HGRN_HEADS, HGRN_DV, HGRN_DK), F32)],
        compiler_params=_cparams("parallel", "arbitrary"),
        name="hgrn",
    )(hz, f_raw, hz, hz, lb_logits, g_hgrn)


def _split_hi_lo(a):
    hi = a.astype(BF16)
    lo = (a - hi.astype(F32)).astype(BF16)
    return hi, lo


def _mix_kernel(oa_ref, or_ref, ga_ref, gr_ref, x_ref, wa_ref, wr_ref, wo_ref, gf_ref, wrt_ref, brt_ref,
                h_ref, xn_ref, lg_ref):
    dot = functools.partial(jnp.dot, preferred_element_type=F32)
    w_hi, w_lo = _split_hi_lo(wrt_ref[...])
    tm = x_ref.shape[0]
    sub = tm // MIX_SUBTILES
    for r in range(MIX_SUBTILES):
        rows = slice(r * sub, (r + 1) * sub)
        ya = dot(oa_ref[rows, :], wa_ref[...])
        yr = dot(or_ref[rows, :], wr_ref[...])
        mix = _sigmoid(ga_ref[rows, :].astype(F32)) * ya + _sigmoid(gr_ref[rows, :].astype(F32)) * yr
        h = x_ref[rows, :] + dot(mix.astype(BF16), wo_ref[...])
        h_ref[rows, :] = h
        xn = _rms(h) * gf_ref[...]
        _store_token_rows(xn_ref, _pack_bf16_pair(xn[:, :HALF], xn[:, HALF:]), first=r * sub)
        x_hi, x_lo = _split_hi_lo(xn)
        lg_ref[rows, :] = dot(x_hi, w_hi) + (dot(x_hi, w_lo) + dot(x_lo, w_hi)) + brt_ref[...]


def _mix_out(o_att, o_rec, hz, x2, w_up_att, w_up_rec, w_out, g_ffn, w_router, b_router, tm):
    n = x2.shape[0]
    const = lambda shape: pl.BlockSpec(shape, lambda i: (0,) * len(shape))
    ga_col = 4 * HGRN_W // D_MODEL
    return pl.pallas_call(
        _mix_kernel,
        grid=(n // tm,),
        in_specs=[
            pl.BlockSpec((tm, ATT_Q_W), lambda i: (i, 0)),
            pl.BlockSpec((tm, HGRN_W), lambda i: (i, 0)),
            pl.BlockSpec((tm, D_MODEL), lambda i: (i, ga_col)),
            pl.BlockSpec((tm, D_MODEL), lambda i: (i, ga_col + 1)),
            pl.BlockSpec((tm, D_MODEL), lambda i: (i, 0)),
            const((ATT_Q_W, D_MODEL)),
            const((HGRN_W, D_MODEL)),
            const((D_MODEL, D_MODEL)),
            const((1, D_MODEL)),
            const((D_MODEL, ROUTER_W)),
            const((1, ROUTER_W)),
        ],
        out_specs=[
            pl.BlockSpec((tm, D_MODEL), lambda i: (i, 0)),
            pl.BlockSpec((tm * ROW_TILE, LANES), lambda i: (i, 0)),
            pl.BlockSpec((tm, ROUTER_W), lambda i: (i, 0)),
        ],
        out_shape=[
            jax.ShapeDtypeStruct((n, D_MODEL), F32),
            jax.ShapeDtypeStruct((n * ROW_TILE, LANES), U32),
            jax.ShapeDtypeStruct((n, ROUTER_W), F32),
        ],
        compiler_params=_cparams("parallel"),
        name="mix_out",
    )(o_att, o_rec, hz, hz, x2, w_up_att, w_up_rec, w_out, g_ffn, w_router, b_router)


def _route_choice(lg):
    tm = lg.shape[0]
    lane = lax.broadcasted_iota(I32, lg.shape, 1)
    neg = jnp.float32(-jnp.inf)
    is_coarse = lane < MOE_GROUPS
    lc = jnp.where(is_coarse, lg, neg)
    cmax = jnp.max(lc, axis=-1, keepdims=True)
    p_sel = 1.0 / jnp.sum(jnp.exp(lc - cmax), axis=-1, keepdims=True)
    grp = jnp.min(jnp.where(lc == cmax, lane, ROUTER_W), axis=-1, keepdims=True)
    fine_lane = lane - MOE_GROUPS
    in_grp = (fine_lane >= grp * MOE_EXPERTS_PER_GROUP) & (fine_lane < (grp + 1) * MOE_EXPERTS_PER_GROUP)
    lf = jnp.where(in_grp, lg, neg)
    v0 = jnp.max(lf, axis=-1, keepdims=True)
    e0 = jnp.min(jnp.where(lf == v0, fine_lane, ROUTER_W), axis=-1, keepdims=True)
    lf1 = jnp.where(fine_lane == e0, neg, lf)
    v1 = jnp.max(lf1, axis=-1, keepdims=True)
    e1 = jnp.min(jnp.where(lf1 == v1, fine_lane, ROUTER_W), axis=-1, keepdims=True)
    t1 = jnp.exp(v1 - v0)
    w0 = p_sel / (1.0 + t1)
    w1 = p_sel * t1 / (1.0 + t1)
    return e0, e1, w0, w1


def _route_kernel(lg_ref, dest_ref, wts_ref, blk_ref, cnt_ref, *, tm, nblk_pad):
    phase = pl.program_id(0)
    i = pl.program_id(1)
    E = MOE_EXPERTS

    @pl.when((phase == 0) & (i == 0))
    def _():
        cnt_ref[...] = jnp.zeros_like(cnt_ref)

    e0, e1, w0, w1 = _route_choice(lg_ref[...])
    elane = lax.broadcasted_iota(I32, (tm, E), 1)
    oh0 = elane == e0
    oh1 = elane == e1
    picks = (oh0 | oh1).astype(F32)

    @pl.when(phase == 0)
    def _():
        cnt_ref[0:1, :] = cnt_ref[0:1, :] + jnp.sum(picks, axis=0, keepdims=True)

    @pl.when(phase == 1)
    def _():
        dot = functools.partial(jnp.dot, preferred_element_type=F32)
        nb = jnp.floor((cnt_ref[0:1, :] + (EXPERT_ROWS - 1)) * (1.0 / EXPERT_ROWS))
        nb_hi = jnp.floor(nb * (1.0 / 32.0))
        nb_lo = nb - 32.0 * nb_hi
        r = lax.broadcasted_iota(I32, (E, E), 0)
        c = lax.broadcasted_iota(I32, (E, E), 1)
        upper = (r < c).astype(F32).astype(BF16)
        incl = (r <= c).astype(F32).astype(BF16)
        nb8_hi = jnp.broadcast_to(nb_hi, (8, E)).astype(BF16)
        nb8_lo = jnp.broadcast_to(nb_lo, (8, E)).astype(BF16)
        start_blk = 32.0 * dot(nb8_hi, upper) + dot(nb8_lo, upper)
        end_blk = 32.0 * dot(nb8_hi, incl) + dot(nb8_lo, incl)

        @pl.when(i == 0)
        def _():
            cnt_ref[1:2, :] = jnp.zeros((1, E), F32)
            bidx = lax.broadcasted_iota(I32, (nblk_pad, E), 0).astype(F32)
            be = jnp.sum((end_blk[0:1, :] <= bidx).astype(F32), axis=-1, keepdims=True)
            be = jnp.minimum(be, float(E - 1))
            lane = lax.broadcasted_iota(I32, blk_ref.shape, 1)
            blk_ref[...] = jnp.where(lane == 1, end_blk[0:1, E - 1:E], be).astype(I32)

        tr = lax.broadcasted_iota(I32, (tm, tm), 0)
        tc = lax.broadcasted_iota(I32, (tm, tm), 1)
        strict = (tc < tr).astype(F32).astype(BF16)
        before = dot(strict, picks.astype(BF16))
        pos = start_blk[0:1, :] * float(EXPERT_ROWS) + cnt_ref[1:2, :] + before
        d0 = jnp.sum(jnp.where(oh0, pos, 0.0), axis=-1, keepdims=True)
        d1 = jnp.sum(jnp.where(oh1, pos, 0.0), axis=-1, keepdims=True)
        cnt_ref[1:2, :] = cnt_ref[1:2, :] + jnp.sum(picks, axis=0, keepdims=True)
        two = lax.broadcasted_iota(I32, (tm, 2), 1)
        dest_ref[...] = jnp.where(two == 0, d0, d1).astype(I32)
        wts_ref[...] = jnp.where(two == 0, w0, w1)


def _route(logits, tm, nblk_pad):
    n = logits.shape[0]
    return pl.pallas_call(
        functools.partial(_route_kernel, tm=tm, nblk_pad=nblk_pad),
        grid=(2, n // tm),
        in_specs=[pl.BlockSpec((tm, ROUTER_W), lambda p, i: (i, 0))],
        out_specs=[
            pl.BlockSpec((tm, 2), lambda p, i: (i * p, 0)),
            pl.BlockSpec((tm, 2), lambda p, i: (i * p, 0)),
            pl.BlockSpec((nblk_pad, 128), lambda p, i: (0, 0)),
        ],
        out_shape=[
            jax.ShapeDtypeStruct((n, 2), I32),
            jax.ShapeDtypeStruct((n, 2), F32),
            jax.ShapeDtypeStruct((nblk_pad, 128), I32),
        ],
        scratch_shapes=[pltpu.VMEM((8, MOE_EXPERTS), F32)],
        compiler_params=_cparams("arbitrary", "arbitrary"),
        name="route",
    )(logits)


ISSUE_UNROLL = 8
COMBINE_SUBTILES = 2
MIX_SUBTILES = 1


def _dispatch_kernel(dest_ref, blk_ref, nused_ref, xn_ref, xs_ref, zbuf, sem, zsem, *, tm, nblk):
    i = pl.program_id(0)
    base = i * tm
    block_rows = EXPERT_ROWS * ROW_TILE

    @pl.when(i == 0)
    def _():
        zbuf[...] = jnp.zeros_like(zbuf)
        last_used = nused_ref[0] - 1

        def needs_clear(j):
            return (j >= last_used) | (blk_ref[j] != blk_ref[jnp.minimum(j + 1, nblk - 1)])

        def clear(j):
            rows = pl.ds(pl.multiple_of(j * block_rows, block_rows), block_rows)
            return pltpu.make_async_copy(zbuf, xs_ref.at[rows, :], zsem)

        def start(j, c):
            @pl.when(needs_clear(j))
            def _():
                clear(j).start()
            return c

        def wait(j, c):
            @pl.when(needs_clear(j))
            def _():
                clear(j).wait()
            return c

        lax.fori_loop(0, nblk, start, 0)
        lax.fori_loop(0, nblk, wait, 0)

    def issue(t, c):
        for k in range(2):
            pltpu.make_async_copy(_token_row(xn_ref, t),
                                  _token_row(xs_ref, dest_ref[2 * (base + t) + k]), sem).start()
        return c

    lax.fori_loop(0, tm, issue, 0, unroll=ISSUE_UNROLL)
    for k in range(2):
        pltpu.make_async_copy(xn_ref, xs_ref.at[pl.ds(0, tm * ROW_TILE), :], sem).wait()


def _dispatch(dest_flat, blk_e, n_used, xn_packed, cap, tm):
    n = xn_packed.shape[0] // ROW_TILE
    nblk = cap // EXPERT_ROWS
    return pl.pallas_call(
        functools.partial(_dispatch_kernel, tm=tm, nblk=nblk),
        grid_spec=pltpu.PrefetchScalarGridSpec(
            num_scalar_prefetch=3,
            grid=(n // tm,),
            in_specs=[pl.BlockSpec((tm * ROW_TILE, LANES), lambda i, d, b, u: (i, 0))],
            out_specs=pl.BlockSpec(memory_space=pl.ANY),
            scratch_shapes=[pltpu.VMEM((EXPERT_ROWS * ROW_TILE, LANES), U32),
                            pltpu.SemaphoreType.DMA, pltpu.SemaphoreType.DMA],
        ),
        out_shape=jax.ShapeDtypeStruct((cap * ROW_TILE, LANES), U32),
        compiler_params=_cparams("arbitrary"),
        name="dispatch",
    )(dest_flat, blk_e, n_used, xn_packed)


def _expert_kernel(blk_ref, nused_ref, xs_ref, wg_hbm, wu_hbm, wd_hbm, y_ref,
                   wg_f, wu_f, wd_f, wg_b, wu_b, wd_b, slot_ref, sems, *, nblk):
    i = pl.program_id(0)
    nused = nused_ref[0]
    active = i < nused
    expert = blk_ref[i]
    new_expert = (i == 0) | (expert != blk_ref[jnp.maximum(i - 1, 0)])

    def fetch(e, slot):
        return [pltpu.make_async_copy(w_hbm.at[e], w_f.at[slot], sems.at[slot])
                for w_hbm, w_f in ((wg_hbm, wg_f), (wu_hbm, wu_f), (wd_hbm, wd_f))]

    @pl.when(jnp.logical_not(active))
    def _():
        y_ref[...] = jnp.zeros_like(y_ref)

    @pl.when(i == 0)
    def _():
        slot_ref[0] = 0
        for c in fetch(expert, 0):
            c.start()

    @pl.when(active & new_expert)
    def _():
        slot = slot_ref[0]
        for c in fetch(expert, slot):
            c.wait()
        wg_b[...] = wg_f[slot].astype(BF16)
        wu_b[...] = wu_f[slot].astype(BF16)
        wd_b[...] = wd_f[slot].astype(BF16)
        nxt = lax.while_loop(lambda j: (j < nused) & (blk_ref[jnp.minimum(j, nblk - 1)] == expert),
                             lambda j: j + 1, i + 1)

        @pl.when(nxt < nused)
        def _():
            for c in fetch(blk_ref[jnp.minimum(nxt, nblk - 1)], 1 - slot):
                c.start()

        slot_ref[0] = 1 - slot

    @pl.when(active)
    def _():
        dot = functools.partial(jnp.dot, preferred_element_type=F32)
        lo, hi = _unpack_bf16_pair(_load_token_rows(xs_ref, EXPERT_ROWS))
        lo = lo.astype(BF16)
        hi = hi.astype(BF16)
        g = dot(lo, wg_b[:HALF, :]) + dot(hi, wg_b[HALF:, :])
        u = dot(lo, wu_b[:HALF, :]) + dot(hi, wu_b[HALF:, :])
        hid = (_silu(g) * u).astype(BF16)
        y = dot(hid, wd_b[...])
        _store_token_rows(y_ref, _pack_bf16_pair(y[:, :HALF], y[:, HALF:]))


def _experts(blk_e, n_used, xs, w_gate, w_up, w_down):
    cap = xs.shape[0] // ROW_TILE
    nblk = cap // EXPERT_ROWS
    row = lambda i, blk, nu: (jnp.minimum(i, nu[0] - 1), 0)
    hbm = pl.BlockSpec(memory_space=pl.ANY)
    up_shape, down_shape = (D_MODEL, MOE_HIDDEN), (MOE_HIDDEN, D_MODEL)
    return pl.pallas_call(
        functools.partial(_expert_kernel, nblk=nblk),
        grid_spec=pltpu.PrefetchScalarGridSpec(
            num_scalar_prefetch=2,
            grid=(nblk,),
            in_specs=[pl.BlockSpec((EXPERT_ROWS * ROW_TILE, LANES), row), hbm, hbm, hbm],
            out_specs=pl.BlockSpec((EXPERT_ROWS * ROW_TILE, LANES), lambda i, blk, nu: (i, 0)),
            scratch_shapes=[
                pltpu.VMEM((2,) + up_shape, F32), pltpu.VMEM((2,) + up_shape, F32),
                pltpu.VMEM((2,) + down_shape, F32),
                pltpu.VMEM(up_shape, BF16), pltpu.VMEM(up_shape, BF16), pltpu.VMEM(down_shape, BF16),
                pltpu.SMEM((1,), I32), pltpu.SemaphoreType.DMA((2,)),
            ],
        ),
        out_shape=jax.ShapeDtypeStruct((cap * ROW_TILE, LANES), U32),
        compiler_params=_cparams("arbitrary"),
        name="experts",
    )(blk_e, n_used, xs, w_gate, w_up, w_down)


def _combine_kernel(dest_ref, h_ref, wts_ref, p_ref, y_ref, wpi_ref, gp_ref, wpg_ref, gfin_ref, o_ref,
                    ybuf, sems, *, tm):
    i = pl.program_id(0)
    slot = lax.rem(i, 2)

    def gather(tile, slot):
        base = tile * tm

        def issue(t, c):
            for k in range(2):
                pltpu.make_async_copy(_token_row(y_ref, dest_ref[2 * (base + t) + k]),
                                      _token_row(ybuf.at[slot, k], t), sems.at[slot]).start()
            return c

        lax.fori_loop(0, tm, issue, 0, unroll=ISSUE_UNROLL)

    @pl.when(i == 0)
    def _():
        gather(0, 0)

    @pl.when(i + 1 < pl.num_programs(0))
    def _():
        gather(i + 1, 1 - slot)

    dot = functools.partial(jnp.dot, preferred_element_type=F32)

    for k in range(2):
        pltpu.make_async_copy(y_ref.at[pl.ds(0, tm * ROW_TILE), :], ybuf.at[slot, k], sems.at[slot]).wait()

    sub = tm // COMBINE_SUBTILES
    for r in range(COMBINE_SUBTILES):
        rows = slice(r * sub, (r + 1) * sub)
        ple = _rms(dot(p_ref[rows, :].astype(BF16), wpi_ref[...])) * gp_ref[...]
        w = wts_ref[rows, :]
        w0 = w[:, 0:1]
        w1 = w[:, 1:2]
        lo0, hi0 = _unpack_bf16_pair(_load_token_rows(ybuf.at[slot, 0], sub, first=r * sub))
        lo1, hi1 = _unpack_bf16_pair(_load_token_rows(ybuf.at[slot, 1], sub, first=r * sub))
        moe = jnp.concatenate([w0 * lo0 + w1 * lo1, w0 * hi0 + w1 * hi1], axis=-1)
        h = h_ref[rows, :] + moe
        gate = _sigmoid(dot(_rms(h).astype(BF16), wpg_ref[...]))
        h = h + ple * gate
        o_ref[rows, :] = _rms(h) * gfin_ref[...]


def _combine_ple(dest_flat, h1, wts, p2, y, w_ple_in, g_ple, w_ple_gate, g_final, tm):
    n = h1.shape[0]
    const = lambda shape: pl.BlockSpec(shape, lambda i, d: (0,) * len(shape))
    return pl.pallas_call(
        functools.partial(_combine_kernel, tm=tm),
        grid_spec=pltpu.PrefetchScalarGridSpec(
            num_scalar_prefetch=1,
            grid=(n // tm,),
            in_specs=[
                pl.BlockSpec((tm, D_MODEL), lambda i, d: (i, 0)),
                pl.BlockSpec((tm, 2), lambda i, d: (i, 0)),
                pl.BlockSpec((tm, PLE_DIM), lambda i, d: (i, 0)),
                pl.BlockSpec(memory_space=pl.ANY),
                const((PLE_DIM, D_MODEL)),
                const((1, D_MODEL)),
                const((D_MODEL, D_MODEL)),
                const((1, D_MODEL)),
            ],
            out_specs=pl.BlockSpec((tm, D_MODEL), lambda i, d: (i, 0)),
            scratch_shapes=[pltpu.VMEM((2, 2, tm * ROW_TILE, LANES), U32), pltpu.SemaphoreType.DMA((2,))],
        ),
        out_shape=jax.ShapeDtypeStruct((n, D_MODEL), F32),
        compiler_params=_cparams("arbitrary"),
        name="combine_ple",
    )(dest_flat, h1, wts, p2, y, w_ple_in, g_ple, w_ple_gate, g_final)


def _tile(n, pref):
    t = min(n, pref)
    assert n % t == 0, (n, t)
    return t


def kernel(x, p, g_mix, w_in, b_qkv, sinks, rel_bias, lb_logits, g_hgrn, w_up_att, w_up_rec, w_out, g_ffn,
           w_coarse, b_coarse, w_fine, b_fine, w_gate, w_up, w_down, w_ple_in, g_ple, w_ple_gate, g_final):
    B, T, D = x.shape
    assert D == D_MODEL and T % ATT_BLOCK == 0 and lb_logits.shape[0] == 2
    n = B * T
    layer = 0
    x2 = x.reshape(n, D)
    row = lambda a: a.reshape(1, -1)

    w_in_b = w_in[layer].astype(BF16)
    qkv = _qkv_proj(x2, row(g_mix[layer]), w_in_b[:, :ATT_QKV_W], row(b_qkv[layer]), _tile(n, 512))
    hz, f_raw = _hz_proj(x2, row(g_mix[layer]), w_in_b[:, ATT_QKV_W:], _tile(n, 1024))

    o_att = _swa(qkv, _bias_table(rel_bias), row(sinks[layer]), B, T)
    o_rec = _hgrn(hz, f_raw, lb_logits, row(g_hgrn[layer]), B, T, _tile(T, 512))

    w_router = jnp.concatenate(
        [w_coarse[layer], w_fine[layer], jnp.zeros((D, ROUTER_W - MOE_GROUPS - MOE_EXPERTS), F32)], axis=1)
    b_router = jnp.concatenate(
        [b_coarse[layer], b_fine[layer], jnp.zeros((ROUTER_W - MOE_GROUPS - MOE_EXPERTS,), F32)]).reshape(1, -1)
    h1, xn_packed, logits = _mix_out(
        o_att, o_rec, hz, x2, w_up_att[layer].astype(BF16), w_up_rec[layer].astype(BF16),
        w_out[layer].astype(BF16), row(g_ffn[layer]), w_router, b_router, _tile(n, 256))

    cap = 2 * n + MOE_EXPERTS * EXPERT_ROWS
    nblk = cap // EXPERT_ROWS
    nblk_pad = (nblk + 7) // 8 * 8
    dest, wts, blk_tab = _route(logits, _tile(n, 512), nblk_pad)
    dest_flat = dest.reshape(2 * n)
    blk_e = blk_tab[:nblk, 0]
    n_used = blk_tab[0, 1].reshape(1)

    xs = _dispatch(dest_flat, blk_e, n_used, xn_packed, cap, _tile(n, 512))
    y = _experts(blk_e, n_used, xs, w_gate[layer], w_up[layer], w_down[layer])
    out = _combine_ple(dest_flat, h1, wts, p[layer].reshape(n, PLE_DIM), y, w_ple_in[layer].astype(BF16),
                       row(g_ple[layer]), w_ple_gate[layer].astype(BF16), row(g_final), _tile(n, 256))
    return out.reshape(B, T, D)
```

```python
import functools
import math

import numpy as np
import jax
import jax.numpy as jnp
from jax import lax
from jax.experimental import pallas as pl
from jax.experimental.pallas import tpu as pltpu

F32 = jnp.float32
BF16 = jnp.bfloat16
I32 = jnp.int32
U32 = jnp.uint32

D_MODEL = 2048
ATT_Q_HEADS = 16
ATT_KV_HEADS = 2
ATT_GROUP = ATT_Q_HEADS // ATT_KV_HEADS
ATT_HEAD_DIM = 64
WINDOW = 128
ATT_BLOCK = 128
ATT_Q_W = ATT_Q_HEADS * ATT_HEAD_DIM
ATT_KV_W = ATT_KV_HEADS * ATT_HEAD_DIM
ATT_QKV_W = ATT_Q_W + 2 * ATT_KV_W
REL_BUCKETS = 32
REL_MAX_DIST = 128
HGRN_HEADS = 8
HGRN_DK = 128
HGRN_DV = 128
HGRN_W = HGRN_HEADS * HGRN_DV
HGRN_CHUNK = 64
MOE_GROUPS = 8
MOE_EXPERTS_PER_GROUP = 8
MOE_EXPERTS = MOE_GROUPS * MOE_EXPERTS_PER_GROUP
MOE_HIDDEN = 512
PLE_DIM = 256
EPS = 1e-6

HALF = D_MODEL // 2
ROUTER_W = 128
EXPERT_ROWS = 256
VMEM_LIMIT = 56 * 1024 * 1024


def _cparams(*sem):
    return pltpu.CompilerParams(dimension_semantics=sem, vmem_limit_bytes=VMEM_LIMIT)


def _rms(xf):
    return xf * lax.rsqrt(jnp.mean(xf * xf, axis=-1, keepdims=True) + EPS)


def _sigmoid(x):
    return 0.5 + 0.5 * jnp.tanh(0.5 * x)


def _silu(x):
    h = 0.5 * x
    return h + h * jnp.tanh(h)


def _pack_bf16_pair(lo_f32, hi_f32):
    lo = lax.bitcast_convert_type(lo_f32.astype(BF16).astype(F32), U32)
    hi = lax.bitcast_convert_type(hi_f32.astype(BF16).astype(F32), U32)
    return hi | (lo >> 16)


def _unpack_bf16_pair(w):
    lo = lax.bitcast_convert_type(w << 16, F32)
    hi = lax.bitcast_convert_type(w & jnp.uint32(0xFFFF0000), F32)
    return lo, hi


LANES = 128
ROW_TILE = HALF // LANES


def _store_token_rows(ref, words, first=0):
    tokens = words.shape[0]
    for c in range(ROW_TILE):
        ref[pl.ds(first * ROW_TILE + c, tokens, stride=ROW_TILE), :] = words[:, c * LANES:(c + 1) * LANES]


def _load_token_rows(ref, tokens, first=0):
    return jnp.concatenate([ref[pl.ds(first * ROW_TILE + c, tokens, stride=ROW_TILE), :]
                            for c in range(ROW_TILE)], axis=-1)


def _token_row(ref, idx):
    return ref.at[pl.ds(pl.multiple_of(idx * ROW_TILE, ROW_TILE), ROW_TILE), :]


def _qkv_kernel(x_ref, g_ref, w_ref, b_ref, o_ref):
    u = (_rms(x_ref[...]) * g_ref[...]).astype(BF16)
    acc = jnp.dot(u, w_ref[...], preferred_element_type=F32)
    o_ref[...] = (acc + b_ref[...]).astype(o_ref.dtype)


def _qkv_proj(x2, g_mix, w_qkv, b_qkv, tm):
    n = x2.shape[0]
    return pl.pallas_call(
        _qkv_kernel,
        grid=(n // tm,),
        in_specs=[
            pl.BlockSpec((tm, D_MODEL), lambda i: (i, 0)),
            pl.BlockSpec((1, D_MODEL), lambda i: (0, 0)),
            pl.BlockSpec((D_MODEL, ATT_QKV_W), lambda i: (0, 0)),
            pl.BlockSpec((1, ATT_QKV_W), lambda i: (0, 0)),
        ],
        out_specs=pl.BlockSpec((tm, ATT_QKV_W), lambda i: (i, 0)),
        out_shape=jax.ShapeDtypeStruct((n, ATT_QKV_W), BF16),
        compiler_params=_cparams("parallel"),
        name="qkv_proj",
    )(x2, g_mix, w_qkv, b_qkv)


HZ_TN = 1024
HZ_COLS = 4 * HGRN_W + 2 * D_MODEL
HZ_F_BLOCK = 1


def _hz_kernel(x_ref, g_ref, w_ref, hz_ref, f_ref, u_scr):
    j = pl.program_id(1)

    @pl.when(j == 0)
    def _():
        u_scr[...] = (_rms(x_ref[...]) * g_ref[...]).astype(BF16)

    acc = jnp.dot(u_scr[...], w_ref[...], preferred_element_type=F32)
    hz_ref[...] = acc.astype(BF16)

    @pl.when(j == HZ_F_BLOCK)
    def _():
        f_ref[...] = acc


def _hz_proj(x2, g_mix, w_hz, tm):
    n = x2.shape[0]
    return pl.pallas_call(
        _hz_kernel,
        grid=(n // tm, HZ_COLS // HZ_TN),
        in_specs=[
            pl.BlockSpec((tm, D_MODEL), lambda i, j: (i, 0)),
            pl.BlockSpec((1, D_MODEL), lambda i, j: (0, 0)),
            pl.BlockSpec((D_MODEL, HZ_TN), lambda i, j: (0, j)),
        ],
        out_specs=[
            pl.BlockSpec((tm, HZ_TN), lambda i, j: (i, j)),
            pl.BlockSpec((tm, HGRN_W), lambda i, j: (i, 0)),
        ],
        out_shape=[
            jax.ShapeDtypeStruct((n, HZ_COLS), BF16),
            jax.ShapeDtypeStruct((n, HGRN_W), F32),
        ],
        scratch_shapes=[pltpu.VMEM((tm, D_MODEL), BF16)],
        compiler_params=_cparams("parallel", "arbitrary"),
        name="hz_proj",
    )(x2, g_mix, w_hz)


def _t5_bucket_table():
    qi = np.arange(ATT_BLOCK)[:, None]
    kj = np.arange(2 * ATT_BLOCK)[None, :]
    dist = qi + ATT_BLOCK - kj
    exact = REL_BUCKETS // 2
    d = np.maximum(dist, 0)
    large = exact + (np.log(np.maximum(d, 1).astype(np.float32) / exact)
                     / math.log(REL_MAX_DIST / exact) * (REL_BUCKETS - exact)).astype(np.int32)
    large = np.minimum(large, REL_BUCKETS - 1)
    return np.where(d < exact, d, large).astype(np.int32)


def _bias_kernel(bucket_ref, rel_ref, o_ref):
    bk = bucket_ref[...]
    qi = lax.broadcasted_iota(I32, bk.shape, 0)
    kj = lax.broadcasted_iota(I32, bk.shape, 1)
    dist = qi + ATT_BLOCK - kj
    in_window = (dist >= 0) & (dist < WINDOW)
    first_block = in_window & (kj >= ATT_BLOCK)
    neg = jnp.float32(-jnp.inf)
    for h in range(ATT_Q_HEADS):
        acc = jnp.zeros(bk.shape, F32)
        for b in range(REL_BUCKETS):
            acc = jnp.where(bk == b, rel_ref[b, h], acc)
        o_ref[0, h] = jnp.where(first_block, acc, neg)
        o_ref[1, h] = jnp.where(in_window, acc, neg)


def _bias_table(rel_bias):
    bucket = jnp.asarray(_t5_bucket_table())
    return pl.pallas_call(
        _bias_kernel,
        in_specs=[
            pl.BlockSpec(memory_space=pltpu.VMEM),
            pl.BlockSpec(memory_space=pltpu.SMEM),
        ],
        out_specs=pl.BlockSpec(memory_space=pltpu.VMEM),
        out_shape=jax.ShapeDtypeStruct((2, ATT_Q_HEADS, ATT_BLOCK, 2 * ATT_BLOCK), F32),
        name="bias_table",
    )(bucket, rel_bias)


def _swa_kernel(q_ref, kvc_ref, kvp_ref, bias_ref, sink_ref, o_ref):
    hd = ATT_HEAD_DIM
    scale = hd ** -0.5
    for kk in range(ATT_KV_HEADS):
        k = jnp.concatenate([kvp_ref[:, hd * kk:hd * (kk + 1)], kvc_ref[:, hd * kk:hd * (kk + 1)]], axis=0)
        v = jnp.concatenate([kvp_ref[:, ATT_KV_W + hd * kk:ATT_KV_W + hd * (kk + 1)],
                             kvc_ref[:, ATT_KV_W + hd * kk:ATT_KV_W + hd * (kk + 1)]], axis=0)
        heads = range(kk * ATT_GROUP, (kk + 1) * ATT_GROUP)
        scores = [lax.dot_general(q_ref[:, hd * h:hd * (h + 1)], k, (((1,), (1,)), ((), ())),
                                  preferred_element_type=F32) for h in heads]
        probs, denoms = [], []
        for h, s in zip(heads, scores):
            s = s * scale + bias_ref[0, h]
            sink = sink_ref[0, h]
            m = jnp.maximum(jnp.max(s, axis=-1, keepdims=True), sink)
            p = jnp.exp(s - m)
            denoms.append(jnp.sum(p, axis=-1, keepdims=True) + jnp.exp(sink - m))
            probs.append(p.astype(BF16))
        outs = [jnp.dot(p, v, preferred_element_type=F32) for p in probs]
        for h, o, denom in zip(heads, outs, denoms):
            o_ref[:, hd * h:hd * (h + 1)] = (o / denom).astype(o_ref.dtype)


def _swa(qkv, bias_tab, sinks, batch, seq):
    n = qkv.shape[0]
    nb = seq // ATT_BLOCK
    kv_col = ATT_Q_W // (2 * ATT_KV_W)

    return pl.pallas_call(
        _swa_kernel,
        grid=(batch, nb),
        in_specs=[
            pl.BlockSpec((ATT_BLOCK, ATT_Q_W), lambda b, i: (b * nb + i, 0)),
            pl.BlockSpec((ATT_BLOCK, 2 * ATT_KV_W), lambda b, i: (b * nb + i, kv_col)),
            pl.BlockSpec((ATT_BLOCK, 2 * ATT_KV_W), lambda b, i: (b * nb + jnp.maximum(i - 1, 0), kv_col)),
            pl.BlockSpec((1, ATT_Q_HEADS, ATT_BLOCK, 2 * ATT_BLOCK), lambda b, i: (jnp.minimum(i, 1), 0, 0, 0)),
            pl.BlockSpec(memory_space=pltpu.SMEM),
        ],
        out_specs=pl.BlockSpec((ATT_BLOCK, ATT_Q_W), lambda b, i: (b * nb + i, 0)),
        out_shape=jax.ShapeDtypeStruct((n, ATT_Q_W), BF16),
        compiler_params=_cparams("parallel", "arbitrary"),
        name="swa",
    )(qkv, qkv, qkv, bias_tab, sinks)


def _cumsum_rows(tri_bf16, g):
    w = g.shape[1]
    g1 = g.astype(BF16)
    r1 = g - g1.astype(F32)
    g2 = r1.astype(BF16)
    g3 = (r1 - g2.astype(F32)).astype(BF16)
    parts = jnp.dot(tri_bf16, jnp.concatenate([g1, g2, g3], axis=1), preferred_element_type=F32)
    return parts[:, :w] + (parts[:, w:2 * w] + parts[:, 2 * w:])


def _hgrn_kernel(q_ref, f_ref, i_ref, g_ref, lbl_ref, gh_ref, o_ref, st_ref, *, chunks):
    C = HGRN_CHUNK
    ref_row = C // 2 - 1

    @pl.when(pl.program_id(1) == 0)
    def _():
        st_ref[...] = jnp.zeros_like(st_ref)

    l = lbl_ref[...]
    e = jnp.exp(l - jnp.max(l, axis=0, keepdims=True))
    lb_all = e[0:1, :] / jnp.sum(e, axis=0, keepdims=True)

    row = lax.broadcasted_iota(I32, (C, C), 0)
    col = lax.broadcasted_iota(I32, (C, C), 1)
    causal = row >= col
    tri = causal.astype(F32).astype(BF16)
    gain = gh_ref[...]

    def chunk(c, carry):
        r0 = pl.multiple_of(c * C, C)
        rows = pl.ds(r0, C)
        f = lb_all + (1.0 - lb_all) * _sigmoid(f_ref[rows, :])
        k = 1.0 - f
        b = _cumsum_rows(tri, jnp.log(f))
        qs = _silu(q_ref[rows, :].astype(F32))
        b_ref = b[ref_row:ref_row + 1, :]
        b_last = b[C - 1:C, :]
        qe_all = (qs * jnp.exp(b)).astype(BF16)
        qa_all = (qs * jnp.exp(b - b_ref)).astype(BF16)
        ka_all = (k * jnp.exp(b_ref - b)).astype(BF16)
        kl_all = (k * jnp.exp(b_last - b)).astype(BF16)
        decay = jnp.exp(b_last)
        gate = _silu(g_ref[rows, :].astype(F32))
        nt = (((1,), (1,)), ((), ()))
        heads = [slice(HGRN_DK * h, HGRN_DK * (h + 1)) for h in range(HGRN_HEADS)]
        vs = [i_ref[rows, cs] for cs in heads]
        sts = [st_ref[h] for h in range(HGRN_HEADS)]
        attn = [lax.dot_general(qa_all[:, cs], ka_all[:, cs], nt, preferred_element_type=F32) for cs in heads]
        inter = [lax.dot_general(qe_all[:, cs], st.astype(BF16), nt, preferred_element_type=F32)
                 for cs, st in zip(heads, sts)]
        upd = [lax.dot_general(v, kl_all[:, cs], (((0,), (0,)), ((), ())), preferred_element_type=F32)
               for cs, v in zip(heads, vs)]
        for h, cs in enumerate(heads):
            st_ref[h] = sts[h] * decay[:, cs] + upd[h]
        intra = [jnp.dot(jnp.where(causal, a, 0.0).astype(BF16), v, preferred_element_type=F32)
                 for a, v in zip(attn, vs)]
        for h, cs in enumerate(heads):
            o = inter[h] + intra[h]
            o_ref[rows, cs] = (_rms(o) * gain * gate[:, cs]).astype(o_ref.dtype)
        return carry

    lax.fori_loop(0, chunks, chunk, 0, unroll=2)


def _hgrn(hz, f_raw, lb_logits, g_hgrn, batch, seq, tc):
    n = hz.shape[0]
    nt = seq // tc
    blk = lambda col: pl.BlockSpec((tc, HGRN_W), lambda b, t, col=col: (b * nt + t, col))
    return pl.pallas_call(
        functools.partial(_hgrn_kernel, chunks=tc // HGRN_CHUNK),
        grid=(batch, nt),
        in_specs=[
            blk(0),
            pl.BlockSpec((tc, HGRN_W), lambda b, t: (b * nt + t, 0)),
            blk(2),
            blk(3),
            pl.BlockSpec((2, HGRN_W), lambda b, t: (0, 0)),
            pl.BlockSpec((1, HGRN_DV), lambda b, t: (0, 0)),
        ],
        out_specs=pl.BlockSpec((tc, HGRN_W), lambda b, t: (b * nt + t, 0)),
        out_shape=jax.ShapeDtypeStruct((n, HGRN_W), BF16),
        scratch_shapes=[pltpu.VMEM((HGRN_HEADS, HGRN_DV, HGRN_DK), F32)],
        compiler_params=_cparams("parallel", "arbitrary"),
        name="hgrn",
    )(hz, f_raw, hz, hz, lb_logits, g_hgrn)


def _split_hi_lo(a):
    hi = a.astype(BF16)
    lo = (a - hi.astype(F32)).astype(BF16)
    return hi, lo


def _mix_kernel(oa_ref, or_ref, ga_ref, gr_ref, x_ref, wa_ref, wr_ref, wo_ref, gf_ref, wrt_ref, brt_ref,
                h_ref, xn_ref, lg_ref):
    dot = functools.partial(jnp.dot, preferred_element_type=F32)
    w_hi, w_lo = _split_hi_lo(wrt_ref[...])
    tm = x_ref.shape[0]
    sub = tm // MIX_SUBTILES
    for r in range(MIX_SUBTILES):
        rows = slice(r * sub, (r + 1) * sub)
        ya = dot(oa_ref[rows, :], wa_ref[...])
        yr = dot(or_ref[rows, :], wr_ref[...])
        mix = _sigmoid(ga_ref[rows, :].astype(F32)) * ya + _sigmoid(gr_ref[rows, :].astype(F32)) * yr
        h = x_ref[rows, :] + dot(mix.astype(BF16), wo_ref[...])
        h_ref[rows, :] = h
        xn = _rms(h) * gf_ref[...]
        _store_token_rows(xn_ref, _pack_bf16_pair(xn[:, :HALF], xn[:, HALF:]), first=r * sub)
        x_hi, x_lo = _split_hi_lo(xn)
        lg_ref[rows, :] = dot(x_hi, w_hi) + (dot(x_hi, w_lo) + dot(x_lo, w_hi)) + brt_ref[...]


def _mix_out(o_att, o_rec, hz, x2, w_up_att, w_up_rec, w_out, g_ffn, w_router, b_router, tm):
    n = x2.shape[0]
    const = lambda shape: pl.BlockSpec(shape, lambda i: (0,) * len(shape))
    ga_col = 4 * HGRN_W // D_MODEL
    return pl.pallas_call(
        _mix_kernel,
        grid=(n // tm,),
        in_specs=[
            pl.BlockSpec((tm, ATT_Q_W), lambda i: (i, 0)),
            pl.BlockSpec((tm, HGRN_W), lambda i: (i, 0)),
            pl.BlockSpec((tm, D_MODEL), lambda i: (i, ga_col)),
            pl.BlockSpec((tm, D_MODEL), lambda i: (i, ga_col + 1)),
            pl.BlockSpec((tm, D_MODEL), lambda i: (i, 0)),
            const((ATT_Q_W, D_MODEL)),
            const((HGRN_W, D_MODEL)),
            const((D_MODEL, D_MODEL)),
            const((1, D_MODEL)),
            const((D_MODEL, ROUTER_W)),
            const((1, ROUTER_W)),
        ],
        out_specs=[
            pl.BlockSpec((tm, D_MODEL), lambda i: (i, 0)),
            pl.BlockSpec((tm * ROW_TILE, LANES), lambda i: (i, 0)),
            pl.BlockSpec((tm, ROUTER_W), lambda i: (i, 0)),
        ],
        out_shape=[
            jax.ShapeDtypeStruct((n, D_MODEL), F32),
            jax.ShapeDtypeStruct((n * ROW_TILE, LANES), U32),
            jax.ShapeDtypeStruct((n, ROUTER_W), F32),
        ],
        compiler_params=_cparams("parallel"),
        name="mix_out",
    )(o_att, o_rec, hz, hz, x2, w_up_att, w_up_rec, w_out, g_ffn, w_router, b_router)


def _route_choice(lgt):
    row = lax.broadcasted_iota(I32, lgt.shape, 0).astype(F32)
    neg = jnp.float32(-jnp.inf)
    far = float(ROUTER_W)
    lc = jnp.where(row < MOE_GROUPS, lgt, neg)
    cmax = jnp.max(lc, axis=0, keepdims=True)
    p_sel = 1.0 / jnp.sum(jnp.exp(lc - cmax), axis=0, keepdims=True)
    grp = jnp.min(jnp.where(lc == cmax, row, far), axis=0, keepdims=True)
    fine = row - MOE_GROUPS
    first = grp * MOE_EXPERTS_PER_GROUP
    lf = jnp.where((fine >= first) & (fine < first + MOE_EXPERTS_PER_GROUP), lgt, neg)
    v0 = jnp.max(lf, axis=0, keepdims=True)
    sel0 = fine == jnp.min(jnp.where(lf == v0, fine, far), axis=0, keepdims=True)
    lf1 = jnp.where(sel0, neg, lf)
    v1 = jnp.max(lf1, axis=0, keepdims=True)
    sel1 = fine == jnp.min(jnp.where(lf1 == v1, fine, far), axis=0, keepdims=True)
    t1 = jnp.exp(v1 - v0)
    w0 = p_sel / (1.0 + t1)
    w1 = p_sel * t1 / (1.0 + t1)
    return sel0, sel1, w0, w1


def _route_kernel(lg_ref, dest_ref, wts_ref, blk_ref, cnt_ref, *, tm, nblk_pad):
    phase = pl.program_id(0)
    i = pl.program_id(1)
    R = ROUTER_W

    @pl.when((phase == 0) & (i == 0))
    def _():
        cnt_ref[...] = jnp.zeros_like(cnt_ref)

    sel0, sel1, w0, w1 = _route_choice(lg_ref[...].T)
    picks = (sel0 | sel1).astype(F32)
    picked = jnp.sum(picks, axis=1, keepdims=True)

    @pl.when(phase == 0)
    def _():
        cnt_ref[0] = cnt_ref[0] + picked

    @pl.when(phase == 1)
    def _():
        dot = functools.partial(jnp.dot, preferred_element_type=F32)
        nb = jnp.floor((cnt_ref[0] + (EXPERT_ROWS - 1)) * (1.0 / EXPERT_ROWS))
        nb_hi = jnp.floor(nb * (1.0 / 32.0))
        nb_lo = nb - 32.0 * nb_hi
        r = lax.broadcasted_iota(I32, (R, R), 0)
        c = lax.broadcasted_iota(I32, (R, R), 1)
        lower = (c < r).astype(F32).astype(BF16)
        start_blk = 32.0 * dot(lower, nb_hi.astype(BF16)) + dot(lower, nb_lo.astype(BF16))

        @pl.when(i == 0)
        def _():
            cnt_ref[1] = jnp.zeros((R, LANES), F32)
            end_col = (start_blk + nb)[:, 0:1]
            erow = lax.broadcasted_iota(I32, (R, nblk_pad), 0)
            is_expert = (erow >= MOE_GROUPS) & (erow < MOE_GROUPS + MOE_EXPERTS)
            bidx = lax.broadcasted_iota(I32, (R, nblk_pad), 1).astype(F32)
            be = jnp.sum((is_expert & (end_col <= bidx)).astype(F32), axis=0, keepdims=True)
            be = jnp.minimum(be, float(MOE_EXPERTS - 1))
            n_used = jnp.sum(nb[:, 0:1], axis=0, keepdims=True)
            out_row = lax.broadcasted_iota(I32, blk_ref.shape, 0)
            blk_ref[...] = jnp.where(out_row == 0, be, n_used).astype(I32)

        tr = lax.broadcasted_iota(I32, (tm, tm), 0)
        tc = lax.broadcasted_iota(I32, (tm, tm), 1)
        earlier = (tr < tc).astype(F32).astype(BF16)
        before = dot(picks.astype(BF16), earlier)
        pos = (start_blk * float(EXPERT_ROWS) + cnt_ref[1])[:, 0:1] + before
        dest_ref[0:1, :] = jnp.sum(jnp.where(sel0, pos, 0.0), axis=0, keepdims=True).astype(I32)
        dest_ref[1:2, :] = jnp.sum(jnp.where(sel1, pos, 0.0), axis=0, keepdims=True).astype(I32)
        cnt_ref[1] = cnt_ref[1] + picked
        wrow = lax.broadcasted_iota(I32, (R, tm), 0)
        wts_ref[...] = jnp.where(wrow == 0, w0, jnp.where(wrow == 1, w1, 0.0)).T


def _route(logits, tm, nblk_pad):
    n = logits.shape[0]
    return pl.pallas_call(
        functools.partial(_route_kernel, tm=tm, nblk_pad=nblk_pad),
        grid=(2, n // tm),
        in_specs=[pl.BlockSpec((tm, ROUTER_W), lambda p, i: (i, 0))],
        out_specs=[
            pl.BlockSpec((2, tm), lambda p, i: (0, i * p)),
            pl.BlockSpec((tm, LANES), lambda p, i: (i * p, 0)),
            pl.BlockSpec((8, nblk_pad), lambda p, i: (0, 0)),
        ],
        out_shape=[
            jax.ShapeDtypeStruct((2, n), I32),
            jax.ShapeDtypeStruct((n, LANES), F32),
            jax.ShapeDtypeStruct((8, nblk_pad), I32),
        ],
        scratch_shapes=[pltpu.VMEM((2, ROUTER_W, LANES), F32)],
        compiler_params=_cparams("arbitrary", "arbitrary"),
        name="route",
    )(logits)


ISSUE_UNROLL = 8
COMBINE_SUBTILES = 2
MIX_SUBTILES = 1


def _dispatch_kernel(dest_ref, blk_ref, nused_ref, xn_ref, xs_ref, zbuf, sem, zsem, *, tm, nblk):
    i = pl.program_id(0)
    base = i * tm
    block_rows = EXPERT_ROWS * ROW_TILE

    @pl.when(i == 0)
    def _():
        zbuf[...] = jnp.zeros_like(zbuf)
        last_used = nused_ref[0] - 1

        def needs_clear(j):
            return (j >= last_used) | (blk_ref[j] != blk_ref[jnp.minimum(j + 1, nblk - 1)])

        def clear(j):
            rows = pl.ds(pl.multiple_of(j * block_rows, block_rows), block_rows)
            return pltpu.make_async_copy(zbuf, xs_ref.at[rows, :], zsem)

        def start(j, c):
            @pl.when(needs_clear(j))
            def _():
                clear(j).start()
            return c

        def wait(j, c):
            @pl.when(needs_clear(j))
            def _():
                clear(j).wait()
            return c

        lax.fori_loop(0, nblk, start, 0)
        lax.fori_loop(0, nblk, wait, 0)

    n_tokens = tm * pl.num_programs(0)

    def issue(t, c):
        for k in range(2):
            pltpu.make_async_copy(_token_row(xn_ref, t),
                                  _token_row(xs_ref, dest_ref[k * n_tokens + base + t]), sem).start(priority=k)
        return c

    lax.fori_loop(0, tm, issue, 0, unroll=ISSUE_UNROLL)
    for k in range(2):
        pltpu.make_async_copy(xn_ref, xs_ref.at[pl.ds(0, tm * ROW_TILE), :], sem).wait()


def _dispatch(dest_flat, blk_e, n_used, xn_packed, cap, tm):
    n = xn_packed.shape[0] // ROW_TILE
    nblk = cap // EXPERT_ROWS
    return pl.pallas_call(
        functools.partial(_dispatch_kernel, tm=tm, nblk=nblk),
        grid_spec=pltpu.PrefetchScalarGridSpec(
            num_scalar_prefetch=3,
            grid=(n // tm,),
            in_specs=[pl.BlockSpec((tm * ROW_TILE, LANES), lambda i, d, b, u: (i, 0))],
            out_specs=pl.BlockSpec(memory_space=pl.ANY),
            scratch_shapes=[pltpu.VMEM((EXPERT_ROWS * ROW_TILE, LANES), U32),
                            pltpu.SemaphoreType.DMA, pltpu.SemaphoreType.DMA],
        ),
        out_shape=jax.ShapeDtypeStruct((cap * ROW_TILE, LANES), U32),
        compiler_params=_cparams("arbitrary"),
        name="dispatch",
    )(dest_flat, blk_e, n_used, xn_packed)


def _expert_kernel(blk_ref, nused_ref, xs_ref, wg_hbm, wu_hbm, wd_hbm, y_ref,
                   wg_f, wu_f, wd_f, wg_b, wu_b, wd_b, slot_ref, sems, *, nblk):
    i = pl.program_id(0)
    nused = nused_ref[0]
    active = i < nused
    expert = blk_ref[i]
    new_expert = (i == 0) | (expert != blk_ref[jnp.maximum(i - 1, 0)])

    def fetch(e, slot):
        return [pltpu.make_async_copy(w_hbm.at[e], w_f.at[slot], sems.at[slot])
                for w_hbm, w_f in ((wg_hbm, wg_f), (wu_hbm, wu_f), (wd_hbm, wd_f))]

    @pl.when(jnp.logical_not(active))
    def _():
        y_ref[...] = jnp.zeros_like(y_ref)

    @pl.when(i == 0)
    def _():
        slot_ref[0] = 0
        for c in fetch(expert, 0):
            c.start()

    @pl.when(active & new_expert)
    def _():
        slot = slot_ref[0]
        for c in fetch(expert, slot):
            c.wait()
        wg_b[...] = wg_f[slot].astype(BF16)
        wu_b[...] = wu_f[slot].astype(BF16)
        wd_b[...] = wd_f[slot].astype(BF16)
        nxt = lax.while_loop(lambda j: (j < nused) & (blk_ref[jnp.minimum(j, nblk - 1)] == expert),
                             lambda j: j + 1, i + 1)

        @pl.when(nxt < nused)
        def _():
            for c in fetch(blk_ref[jnp.minimum(nxt, nblk - 1)], 1 - slot):
                c.start()

        slot_ref[0] = 1 - slot

    @pl.when(active)
    def _():
        dot = functools.partial(jnp.dot, preferred_element_type=F32)
        lo, hi = _unpack_bf16_pair(_load_token_rows(xs_ref, EXPERT_ROWS))
        lo = lo.astype(BF16)
        hi = hi.astype(BF16)
        g = dot(lo, wg_b[:HALF, :]) + dot(hi, wg_b[HALF:, :])
        u = dot(lo, wu_b[:HALF, :]) + dot(hi, wu_b[HALF:, :])
        hid = (_silu(g) * u).astype(BF16)
        y = dot(hid, wd_b[...])
        _store_token_rows(y_ref, _pack_bf16_pair(y[:, :HALF], y[:, HALF:]))


def _experts(blk_e, n_used, xs, w_gate, w_up, w_down):
    cap = xs.shape[0] // ROW_TILE
    nblk = cap // EXPERT_ROWS
    row = lambda i, blk, nu: (jnp.minimum(i, nu[0] - 1), 0)
    hbm = pl.BlockSpec(memory_space=pl.ANY)
    up_shape, down_shape = (D_MODEL, MOE_HIDDEN), (MOE_HIDDEN, D_MODEL)
    return pl.pallas_call(
        functools.partial(_expert_kernel, nblk=nblk),
        grid_spec=pltpu.PrefetchScalarGridSpec(
            num_scalar_prefetch=2,
            grid=(nblk,),
            in_specs=[pl.BlockSpec((EXPERT_ROWS * ROW_TILE, LANES), row), hbm, hbm, hbm],
            out_specs=pl.BlockSpec((EXPERT_ROWS * ROW_TILE, LANES), lambda i, blk, nu: (i, 0)),
            scratch_shapes=[
                pltpu.VMEM((2,) + up_shape, F32), pltpu.VMEM((2,) + up_shape, F32),
                pltpu.VMEM((2,) + down_shape, F32),
                pltpu.VMEM(up_shape, BF16), pltpu.VMEM(up_shape, BF16), pltpu.VMEM(down_shape, BF16),
                pltpu.SMEM((1,), I32), pltpu.SemaphoreType.DMA((2,)),
            ],
        ),
        out_shape=jax.ShapeDtypeStruct((cap * ROW_TILE, LANES), U32),
        compiler_params=_cparams("arbitrary"),
        name="experts",
    )(blk_e, n_used, xs, w_gate, w_up, w_down)


def _combine_kernel(dest_ref, h_ref, wts_ref, p_ref, y_ref, wpi_ref, gp_ref, wpg_ref, gfin_ref, o_ref,
                    ybuf, sems, *, tm):
    i = pl.program_id(0)
    slot = lax.rem(i, 2)

    n_tokens = tm * pl.num_programs(0)

    def gather(tile, slot):
        base = tile * tm

        def issue(t, c):
            for k in range(2):
                pltpu.make_async_copy(_token_row(y_ref, dest_ref[k * n_tokens + base + t]),
                                      _token_row(ybuf.at[slot, k], t), sems.at[slot]).start(priority=1)
            return c

        lax.fori_loop(0, tm, issue, 0, unroll=ISSUE_UNROLL)

    @pl.when(i == 0)
    def _():
        gather(0, 0)

    @pl.when(i + 1 < pl.num_programs(0))
    def _():
        gather(i + 1, 1 - slot)

    dot = functools.partial(jnp.dot, preferred_element_type=F32)

    for k in range(2):
        pltpu.make_async_copy(y_ref.at[pl.ds(0, tm * ROW_TILE), :], ybuf.at[slot, k], sems.at[slot]).wait()

    sub = tm // COMBINE_SUBTILES
    for r in range(COMBINE_SUBTILES):
        rows = slice(r * sub, (r + 1) * sub)
        ple = _rms(dot(p_ref[rows, :].astype(BF16), wpi_ref[...])) * gp_ref[...]
        w = wts_ref[rows, :]
        w0 = w[:, 0:1]
        w1 = w[:, 1:2]
        lo0, hi0 = _unpack_bf16_pair(_load_token_rows(ybuf.at[slot, 0], sub, first=r * sub))
        lo1, hi1 = _unpack_bf16_pair(_load_token_rows(ybuf.at[slot, 1], sub, first=r * sub))
        moe = jnp.concatenate([w0 * lo0 + w1 * lo1, w0 * hi0 + w1 * hi1], axis=-1)
        h = h_ref[rows, :] + moe
        gate = _sigmoid(dot(_rms(h).astype(BF16), wpg_ref[...]))
        h = h + ple * gate
        o_ref[rows, :] = _rms(h) * gfin_ref[...]


def _combine_ple(dest_flat, h1, wts, p2, y, w_ple_in, g_ple, w_ple_gate, g_final, tm):
    n = h1.shape[0]
    const = lambda shape: pl.BlockSpec(shape, lambda i, d: (0,) * len(shape))
    return pl.pallas_call(
        functools.partial(_combine_kernel, tm=tm),
        grid_spec=pltpu.PrefetchScalarGridSpec(
            num_scalar_prefetch=1,
            grid=(n // tm,),
            in_specs=[
                pl.BlockSpec((tm, D_MODEL), lambda i, d: (i, 0)),
                pl.BlockSpec((tm, LANES), lambda i, d: (i, 0)),
                pl.BlockSpec((tm, PLE_DIM), lambda i, d: (i, 0)),
                pl.BlockSpec(memory_space=pl.ANY),
                const((PLE_DIM, D_MODEL)),
                const((1, D_MODEL)),
                const((D_MODEL, D_MODEL)),
                const((1, D_MODEL)),
            ],
            out_specs=pl.BlockSpec((tm, D_MODEL), lambda i, d: (i, 0)),
            scratch_shapes=[pltpu.VMEM((2, 2, tm * ROW_TILE, LANES), U32), pltpu.SemaphoreType.DMA((2,))],
        ),
        out_shape=jax.ShapeDtypeStruct((n, D_MODEL), F32),
        compiler_params=_cparams("arbitrary"),
        name="combine_ple",
    )(dest_flat, h1, wts, p2, y, w_ple_in, g_ple, w_ple_gate, g_final)


def _tile(n, pref):
    t = min(n, pref)
    assert n % t == 0, (n, t)
    return t


def kernel(x, p, g_mix, w_in, b_qkv, sinks, rel_bias, lb_logits, g_hgrn, w_up_att, w_up_rec, w_out, g_ffn,
           w_coarse, b_coarse, w_fine, b_fine, w_gate, w_up, w_down, w_ple_in, g_ple, w_ple_gate, g_final):
    B, T, D = x.shape
    assert D == D_MODEL and T % ATT_BLOCK == 0 and lb_logits.shape[0] == 2
    n = B * T
    layer = 0
    x2 = x.reshape(n, D)
    row = lambda a: a.reshape(1, -1)

    w_in_b = w_in[layer].astype(BF16)
    qkv = _qkv_proj(x2, row(g_mix[layer]), w_in_b[:, :ATT_QKV_W], row(b_qkv[layer]), _tile(n, 512))
    hz, f_raw = _hz_proj(x2, row(g_mix[layer]), w_in_b[:, ATT_QKV_W:], _tile(n, 1024))

    o_att = _swa(qkv, _bias_table(rel_bias), row(sinks[layer]), B, T)
    o_rec = _hgrn(hz, f_raw, lb_logits, row(g_hgrn[layer]), B, T, _tile(T, 512))

    w_router = jnp.concatenate(
        [w_coarse[layer], w_fine[layer], jnp.zeros((D, ROUTER_W - MOE_GROUPS - MOE_EXPERTS), F32)], axis=1)
    b_router = jnp.concatenate(
        [b_coarse[layer], b_fine[layer], jnp.zeros((ROUTER_W - MOE_GROUPS - MOE_EXPERTS,), F32)]).reshape(1, -1)
    h1, xn_packed, logits = _mix_out(
        o_att, o_rec, hz, x2, w_up_att[layer].astype(BF16), w_up_rec[layer].astype(BF16),
        w_out[layer].astype(BF16), row(g_ffn[layer]), w_router, b_router, _tile(n, 256))

    cap = 2 * n + MOE_EXPERTS * EXPERT_ROWS
    nblk = cap // EXPERT_ROWS
    nblk_pad = (nblk + LANES - 1) // LANES * LANES
    dest, wts, blk_tab = _route(logits, _tile(n, 512), nblk_pad)
    dest_flat = dest.reshape(2 * n)
    blk_e = blk_tab[0, :nblk]
    n_used = blk_tab[1, 0:1]

    xs = _dispatch(dest_flat, blk_e, n_used, xn_packed, cap, _tile(n, 512))
    y = _experts(blk_e, n_used, xs, w_gate[layer], w_up[layer], w_down[layer])
    out = _combine_ple(dest_flat, h1, wts, p[layer].reshape(n, PLE_DIM), y, w_ple_in[layer].astype(BF16),
                       row(g_ple[layer]), w_ple_gate[layer].astype(BF16), row(g_final), _tile(n, 256))
    return out.reshape(B, T, D)
```

```python
import functools
import math

import numpy as np
import jax
import jax.numpy as jnp
from jax import lax
from jax.experimental import pallas as pl
from jax.experimental.pallas import tpu as pltpu

F32 = jnp.float32
BF16 = jnp.bfloat16
I32 = jnp.int32
U32 = jnp.uint32

D_MODEL = 2048
ATT_Q_HEADS = 16
ATT_KV_HEADS = 2
ATT_GROUP = ATT_Q_HEADS // ATT_KV_HEADS
ATT_HEAD_DIM = 64
WINDOW = 128
ATT_BLOCK = 128
ATT_Q_W = ATT_Q_HEADS * ATT_HEAD_DIM
ATT_KV_W = ATT_KV_HEADS * ATT_HEAD_DIM
ATT_QKV_W = ATT_Q_W + 2 * ATT_KV_W
REL_BUCKETS = 32
REL_MAX_DIST = 128
HGRN_HEADS = 8
HGRN_DK = 128
HGRN_DV = 128
HGRN_W = HGRN_HEADS * HGRN_DV
HGRN_CHUNK = 64
MOE_GROUPS = 8
MOE_EXPERTS_PER_GROUP = 8
MOE_EXPERTS = MOE_GROUPS * MOE_EXPERTS_PER_GROUP
MOE_HIDDEN = 512
PLE_DIM = 256
EPS = 1e-6

HALF = D_MODEL // 2
ROUTER_W = 128
EXPERT_ROWS = 256
VMEM_LIMIT = 56 * 1024 * 1024


def _cparams(*sem):
    return pltpu.CompilerParams(dimension_semantics=sem, vmem_limit_bytes=VMEM_LIMIT)


def _rms(xf):
    return xf * lax.rsqrt(jnp.mean(xf * xf, axis=-1, keepdims=True) + EPS)


def _sigmoid(x):
    return 0.5 + 0.5 * jnp.tanh(0.5 * x)


def _silu(x):
    h = 0.5 * x
    return h + h * jnp.tanh(h)


def _pack_bf16_pair(lo_f32, hi_f32):
    lo = lax.bitcast_convert_type(lo_f32.astype(BF16).astype(F32), U32)
    hi = lax.bitcast_convert_type(hi_f32.astype(BF16).astype(F32), U32)
    return hi | (lo >> 16)


def _unpack_bf16_pair(w):
    lo = lax.bitcast_convert_type(w << 16, F32)
    hi = lax.bitcast_convert_type(w & jnp.uint32(0xFFFF0000), F32)
    return lo, hi


LANES = 128
ROW_TILE = HALF // LANES


def _store_token_rows(ref, words, first=0):
    tokens = words.shape[0]
    for c in range(ROW_TILE):
        ref[pl.ds(first * ROW_TILE + c, tokens, stride=ROW_TILE), :] = words[:, c * LANES:(c + 1) * LANES]


def _load_token_rows(ref, tokens, first=0):
    return jnp.concatenate([ref[pl.ds(first * ROW_TILE + c, tokens, stride=ROW_TILE), :]
                            for c in range(ROW_TILE)], axis=-1)


def _token_row(ref, idx):
    return ref.at[pl.ds(pl.multiple_of(idx * ROW_TILE, ROW_TILE), ROW_TILE), :]


def _qkv_kernel(x_ref, g_ref, w_ref, b_ref, o_ref):
    u = (_rms(x_ref[...]) * g_ref[...]).astype(BF16)
    acc = jnp.dot(u, w_ref[...], preferred_element_type=F32)
    o_ref[...] = (acc + b_ref[...]).astype(o_ref.dtype)


def _qkv_proj(x2, g_mix, w_qkv, b_qkv, tm):
    n = x2.shape[0]
    return pl.pallas_call(
        _qkv_kernel,
        grid=(n // tm,),
        in_specs=[
            pl.BlockSpec((tm, D_MODEL), lambda i: (i, 0)),
            pl.BlockSpec((1, D_MODEL), lambda i: (0, 0)),
            pl.BlockSpec((D_MODEL, ATT_QKV_W), lambda i: (0, 0)),
            pl.BlockSpec((1, ATT_QKV_W), lambda i: (0, 0)),
        ],
        out_specs=pl.BlockSpec((tm, ATT_QKV_W), lambda i: (i, 0)),
        out_shape=jax.ShapeDtypeStruct((n, ATT_QKV_W), BF16),
        compiler_params=_cparams("parallel"),
        name="qkv_proj",
    )(x2, g_mix, w_qkv, b_qkv)


HZ_TN = 1024
HZ_COLS = 4 * HGRN_W + 2 * D_MODEL
HZ_F_BLOCK = 1


def _hz_kernel(x_ref, g_ref, w_ref, hz_ref, f_ref, u_scr):
    j = pl.program_id(1)

    @pl.when(j == 0)
    def _():
        u_scr[...] = (_rms(x_ref[...]) * g_ref[...]).astype(BF16)

    acc = jnp.dot(u_scr[...], w_ref[...], preferred_element_type=F32)
    hz_ref[...] = acc.astype(BF16)

    @pl.when(j == HZ_F_BLOCK)
    def _():
        f_ref[...] = acc


def _hz_proj(x2, g_mix, w_hz, tm):
    n = x2.shape[0]
    return pl.pallas_call(
        _hz_kernel,
        grid=(n // tm, HZ_COLS // HZ_TN),
        in_specs=[
            pl.BlockSpec((tm, D_MODEL), lambda i, j: (i, 0)),
            pl.BlockSpec((1, D_MODEL), lambda i, j: (0, 0)),
            pl.BlockSpec((D_MODEL, HZ_TN), lambda i, j: (0, j)),
        ],
        out_specs=[
            pl.BlockSpec((tm, HZ_TN), lambda i, j: (i, j)),
            pl.BlockSpec((tm, HGRN_W), lambda i, j: (i, 0)),
        ],
        out_shape=[
            jax.ShapeDtypeStruct((n, HZ_COLS), BF16),
            jax.ShapeDtypeStruct((n, HGRN_W), F32),
        ],
        scratch_shapes=[pltpu.VMEM((tm, D_MODEL), BF16)],
        compiler_params=_cparams("parallel", "arbitrary"),
        name="hz_proj",
    )(x2, g_mix, w_hz)


def _t5_bucket_table():
    qi = np.arange(ATT_BLOCK)[:, None]
    kj = np.arange(2 * ATT_BLOCK)[None, :]
    dist = qi + ATT_BLOCK - kj
    exact = REL_BUCKETS // 2
    d = np.maximum(dist, 0)
    large = exact + (np.log(np.maximum(d, 1).astype(np.float32) / exact)
                     / math.log(REL_MAX_DIST / exact) * (REL_BUCKETS - exact)).astype(np.int32)
    large = np.minimum(large, REL_BUCKETS - 1)
    return np.where(d < exact, d, large).astype(np.int32)


def _bias_kernel(bucket_ref, rel_ref, o_ref):
    bk = bucket_ref[...]
    qi = lax.broadcasted_iota(I32, bk.shape, 0)
    kj = lax.broadcasted_iota(I32, bk.shape, 1)
    dist = qi + ATT_BLOCK - kj
    in_window = (dist >= 0) & (dist < WINDOW)
    first_block = in_window & (kj >= ATT_BLOCK)
    neg = jnp.float32(-jnp.inf)
    for h in range(ATT_Q_HEADS):
        acc = jnp.zeros(bk.shape, F32)
        for b in range(REL_BUCKETS):
            acc = jnp.where(bk == b, rel_ref[b, h], acc)
        o_ref[0, h] = jnp.where(first_block, acc, neg)
        o_ref[1, h] = jnp.where(in_window, acc, neg)


def _bias_table(rel_bias):
    bucket = jnp.asarray(_t5_bucket_table())
    return pl.pallas_call(
        _bias_kernel,
        in_specs=[
            pl.BlockSpec(memory_space=pltpu.VMEM),
            pl.BlockSpec(memory_space=pltpu.SMEM),
        ],
        out_specs=pl.BlockSpec(memory_space=pltpu.VMEM),
        out_shape=jax.ShapeDtypeStruct((2, ATT_Q_HEADS, ATT_BLOCK, 2 * ATT_BLOCK), F32),
        name="bias_table",
    )(bucket, rel_bias)


SWA_BLOCKS = 2


def _swa_kernel(q_ref, kvc_ref, kvp_ref, bias_ref, sink_ref, o_ref):
    hd = ATT_HEAD_DIM
    blk = ATT_BLOCK
    nt = (((1,), (1,)), ((), ()))
    step = pl.program_id(1)
    left = lax.broadcasted_iota(I32, (2 * blk, 2 * hd), 1) < hd
    left_q = lax.broadcasted_iota(I32, (blk, 2 * hd), 1) < hd
    one = jnp.ones((), F32)
    zero = jnp.zeros((), F32)
    for sb in range(SWA_BLOCKS):
        rows = slice(sb * blk, (sb + 1) * blk)
        prev = kvp_ref[...] if sb == 0 else kvc_ref[(sb - 1) * blk:sb * blk, :]
        band = jnp.concatenate([prev, kvc_ref[rows, :]], axis=0).astype(F32)
        kcol = band[:, :2 * hd] * (hd ** -0.5)
        vcol = band[:, 2 * hd:]
        krot = pltpu.roll(kcol, hd, axis=1)
        vrot = pltpu.roll(vcol, hd, axis=1)
        table = jnp.minimum(step * SWA_BLOCKS + sb, 1) if sb == 0 else 1
        for kk in range(ATT_KV_HEADS):
            k_src, k_alt = (kcol, krot) if kk == 0 else (krot, kcol)
            v_src, v_alt = (vcol, vrot) if kk == 0 else (vrot, vcol)
            k_a = jnp.where(left, k_src, zero).astype(BF16)
            k_b = jnp.where(left, zero, k_alt).astype(BF16)
            v_a = jnp.where(left, v_src, one).astype(BF16)
            v_b = jnp.where(left, one, v_alt).astype(BF16)
            pairs = range(kk * ATT_GROUP // 2, (kk + 1) * ATT_GROUP // 2)
            scores = []
            for j in pairs:
                qp = q_ref[rows, 2 * hd * j:2 * hd * (j + 1)]
                scores.append((lax.dot_general(qp, k_a, nt, preferred_element_type=F32),
                               lax.dot_general(qp, k_b, nt, preferred_element_type=F32)))
            probs, sink_terms = [], []
            for j, pair in zip(pairs, scores):
                p_pair, t_pair = [], []
                for h, s in zip((2 * j, 2 * j + 1), pair):
                    s = s + bias_ref[table, h]
                    sink = sink_ref[0, h]
                    m = jnp.maximum(jnp.max(s, axis=-1, keepdims=True), sink)
                    p_pair.append(jnp.exp(s - m).astype(BF16))
                    t_pair.append(jnp.exp(sink - m))
                probs.append(p_pair)
                sink_terms.append(t_pair)
            ext = [(jnp.dot(pa, v_a, preferred_element_type=F32), jnp.dot(pb, v_b, preferred_element_type=F32))
                   for pa, pb in probs]
            for j, (ea, eb), (ta, tb) in zip(pairs, ext, sink_terms):
                num = jnp.where(left_q, ea, eb)
                den = pltpu.roll(jnp.where(left_q, eb, ea), hd, axis=1)
                den = den + jnp.where(left_q, ta, tb)
                o_ref[rows, 2 * hd * j:2 * hd * (j + 1)] = (num / den).astype(o_ref.dtype)


def _swa(qkv, bias_tab, sinks, batch, seq):
    n = qkv.shape[0]
    nb = seq // ATT_BLOCK
    assert nb % SWA_BLOCKS == 0
    ns = nb // SWA_BLOCKS
    rows = SWA_BLOCKS * ATT_BLOCK
    kv_col = ATT_Q_W // (2 * ATT_KV_W)

    return pl.pallas_call(
        _swa_kernel,
        grid=(batch, ns),
        in_specs=[
            pl.BlockSpec((rows, ATT_Q_W), lambda b, i: (b * ns + i, 0)),
            pl.BlockSpec((rows, 2 * ATT_KV_W), lambda b, i: (b * ns + i, kv_col)),
            pl.BlockSpec((ATT_BLOCK, 2 * ATT_KV_W),
                         lambda b, i: (b * nb + jnp.maximum(i * SWA_BLOCKS - 1, 0), kv_col)),
            pl.BlockSpec((2, ATT_Q_HEADS, ATT_BLOCK, 2 * ATT_BLOCK), lambda b, i: (0, 0, 0, 0)),
            pl.BlockSpec(memory_space=pltpu.SMEM),
        ],
        out_specs=pl.BlockSpec((rows, ATT_Q_W), lambda b, i: (b * ns + i, 0)),
        out_shape=jax.ShapeDtypeStruct((n, ATT_Q_W), BF16),
        compiler_params=_cparams("parallel", "arbitrary"),
        name="swa",
    )(qkv, qkv, qkv, bias_tab, sinks)


def _cumsum_rows(tri_bf16, g):
    w = g.shape[1]
    g1 = g.astype(BF16)
    r1 = g - g1.astype(F32)
    g2 = r1.astype(BF16)
    g3 = (r1 - g2.astype(F32)).astype(BF16)
    parts = jnp.dot(tri_bf16, jnp.concatenate([g1, g2, g3], axis=1), preferred_element_type=F32)
    return parts[:, :w] + (parts[:, w:2 * w] + parts[:, 2 * w:])


def _hgrn_kernel(q_ref, f_ref, i_ref, g_ref, lbl_ref, gh_ref, o_ref, st_ref, *, chunks):
    C = HGRN_CHUNK
    ref_row = C // 2 - 1

    @pl.when(pl.program_id(1) == 0)
    def _():
        st_ref[...] = jnp.zeros_like(st_ref)

    l = lbl_ref[...]
    e = jnp.exp(l - jnp.max(l, axis=0, keepdims=True))
    lb_all = e[0:1, :] / jnp.sum(e, axis=0, keepdims=True)

    row = lax.broadcasted_iota(I32, (C, C), 0)
    col = lax.broadcasted_iota(I32, (C, C), 1)
    causal = row >= col
    tri = causal.astype(F32).astype(BF16)
    gain = gh_ref[...]

    def chunk(c, carry):
        r0 = pl.multiple_of(c * C, C)
        rows = pl.ds(r0, C)
        f = lb_all + (1.0 - lb_all) * _sigmoid(f_ref[rows, :])
        k = 1.0 - f
        b = _cumsum_rows(tri, jnp.log(f))
        qs = _silu(q_ref[rows, :].astype(F32))
        b_ref = b[ref_row:ref_row + 1, :]
        b_last = b[C - 1:C, :]
        qe_all = (qs * jnp.exp(b)).astype(BF16)
        qa_all = (qs * jnp.exp(b - b_ref)).astype(BF16)
        ka_all = (k * jnp.exp(b_ref - b)).astype(BF16)
        kl_all = (k * jnp.exp(b_last - b)).astype(BF16)
        decay = jnp.exp(b_last)
        gate = _silu(g_ref[rows, :].astype(F32))
        nt = (((1,), (1,)), ((), ()))
        heads = [slice(HGRN_DK * h, HGRN_DK * (h + 1)) for h in range(HGRN_HEADS)]
        vs = [i_ref[rows, cs] for cs in heads]
        sts = [st_ref[h] for h in range(HGRN_HEADS)]
        attn = [lax.dot_general(qa_all[:, cs], ka_all[:, cs], nt, preferred_element_type=F32) for cs in heads]
        inter = [lax.dot_general(qe_all[:, cs], st.astype(BF16), nt, preferred_element_type=F32)
                 for cs, st in zip(heads, sts)]
        upd = [lax.dot_general(v, kl_all[:, cs], (((0,), (0,)), ((), ())), preferred_element_type=F32)
               for cs, v in zip(heads, vs)]
        for h, cs in enumerate(heads):
            st_ref[h] = sts[h] * decay[:, cs] + upd[h]
        intra = [jnp.dot(jnp.where(causal, a, 0.0).astype(BF16), v, preferred_element_type=F32)
                 for a, v in zip(attn, vs)]
        for h, cs in enumerate(heads):
            o = inter[h] + intra[h]
            o_ref[rows, cs] = (_rms(o) * gain * gate[:, cs]).astype(o_ref.dtype)
        return carry

    lax.fori_loop(0, chunks, chunk, 0, unroll=2)


def _hgrn(hz, f_raw, lb_logits, g_hgrn, batch, seq, tc):
    n = hz.shape[0]
    nt = seq // tc
    blk = lambda col: pl.BlockSpec((tc, HGRN_W), lambda b, t, col=col: (b * nt + t, col))
    return pl.pallas_call(
        functools.partial(_hgrn_kernel, chunks=tc // HGRN_CHUNK),
        grid=(batch, nt),
        in_specs=[
            blk(0),
            pl.BlockSpec((tc, HGRN_W), lambda b, t: (b * nt + t, 0)),
            blk(2),
            blk(3),
            pl.BlockSpec((2, HGRN_W), lambda b, t: (0, 0)),
            pl.BlockSpec((1, HGRN_DV), lambda b, t: (0, 0)),
        ],
        out_specs=pl.BlockSpec((tc, HGRN_W), lambda b, t: (b * nt + t, 0)),
        out_shape=jax.ShapeDtypeStruct((n, HGRN_W), BF16),
        scratch_shapes=[pltpu.VMEM((HGRN_HEADS, HGRN_DV, HGRN_DK), F32)],
        compiler_params=_cparams("parallel", "arbitrary"),
        name="hgrn",
    )(hz, f_raw, hz, hz, lb_logits, g_hgrn)


def _split_hi_lo(a):
    hi = a.astype(BF16)
    lo = (a - hi.astype(F32)).astype(BF16)
    return hi, lo


def _mix_kernel(oa_ref, or_ref, ga_ref, gr_ref, x_ref, wa_ref, wr_ref, wo_ref, gf_ref, wrt_ref, brt_ref,
                h_ref, xn_ref, lg_ref):
    dot = functools.partial(jnp.dot, preferred_element_type=F32)
    w_hi, w_lo = _split_hi_lo(wrt_ref[...])
    tm = x_ref.shape[0]
    sub = tm // MIX_SUBTILES
    for r in range(MIX_SUBTILES):
        rows = slice(r * sub, (r + 1) * sub)
        ya = dot(oa_ref[rows, :], wa_ref[...])
        yr = dot(or_ref[rows, :], wr_ref[...])
        mix = _sigmoid(ga_ref[rows, :].astype(F32)) * ya + _sigmoid(gr_ref[rows, :].astype(F32)) * yr
        h = x_ref[rows, :] + dot(mix.astype(BF16), wo_ref[...])
        h_ref[rows, :] = h
        xn = _rms(h) * gf_ref[...]
        _store_token_rows(xn_ref, _pack_bf16_pair(xn[:, :HALF], xn[:, HALF:]), first=r * sub)
        x_hi, x_lo = _split_hi_lo(xn)
        lg_ref[rows, :] = dot(x_hi, w_hi) + (dot(x_hi, w_lo) + dot(x_lo, w_hi)) + brt_ref[...]


def _mix_out(o_att, o_rec, hz, x2, w_up_att, w_up_rec, w_out, g_ffn, w_router, b_router, tm):
    n = x2.shape[0]
    const = lambda shape: pl.BlockSpec(shape, lambda i: (0,) * len(shape))
    ga_col = 4 * HGRN_W // D_MODEL
    return pl.pallas_call(
        _mix_kernel,
        grid=(n // tm,),
        in_specs=[
            pl.BlockSpec((tm, ATT_Q_W), lambda i: (i, 0)),
            pl.BlockSpec((tm, HGRN_W), lambda i: (i, 0)),
            pl.BlockSpec((tm, D_MODEL), lambda i: (i, ga_col)),
            pl.BlockSpec((tm, D_MODEL), lambda i: (i, ga_col + 1)),
            pl.BlockSpec((tm, D_MODEL), lambda i: (i, 0)),
            const((ATT_Q_W, D_MODEL)),
            const((HGRN_W, D_MODEL)),
            const((D_MODEL, D_MODEL)),
            const((1, D_MODEL)),
            const((D_MODEL, ROUTER_W)),
            const((1, ROUTER_W)),
        ],
        out_specs=[
            pl.BlockSpec((tm, D_MODEL), lambda i: (i, 0)),
            pl.BlockSpec((tm * ROW_TILE, LANES), lambda i: (i, 0)),
            pl.BlockSpec((tm, ROUTER_W), lambda i: (i, 0)),
        ],
        out_shape=[
            jax.ShapeDtypeStruct((n, D_MODEL), F32),
            jax.ShapeDtypeStruct((n * ROW_TILE, LANES), U32),
            jax.ShapeDtypeStruct((n, ROUTER_W), F32),
        ],
        compiler_params=_cparams("parallel"),
        name="mix_out",
    )(o_att, o_rec, hz, hz, x2, w_up_att, w_up_rec, w_out, g_ffn, w_router, b_router)


def _route_choice(lgt):
    row = lax.broadcasted_iota(I32, lgt.shape, 0).astype(F32)
    neg = jnp.float32(-jnp.inf)
    far = float(ROUTER_W)
    lc = jnp.where(row < MOE_GROUPS, lgt, neg)
    cmax = jnp.max(lc, axis=0, keepdims=True)
    p_sel = 1.0 / jnp.sum(jnp.exp(lc - cmax), axis=0, keepdims=True)
    grp = jnp.min(jnp.where(lc == cmax, row, far), axis=0, keepdims=True)
    fine = row - MOE_GROUPS
    first = grp * MOE_EXPERTS_PER_GROUP
    lf = jnp.where((fine >= first) & (fine < first + MOE_EXPERTS_PER_GROUP), lgt, neg)
    v0 = jnp.max(lf, axis=0, keepdims=True)
    sel0 = fine == jnp.min(jnp.where(lf == v0, fine, far), axis=0, keepdims=True)
    lf1 = jnp.where(sel0, neg, lf)
    v1 = jnp.max(lf1, axis=0, keepdims=True)
    sel1 = fine == jnp.min(jnp.where(lf1 == v1, fine, far), axis=0, keepdims=True)
    t1 = jnp.exp(v1 - v0)
    w0 = p_sel / (1.0 + t1)
    w1 = p_sel * t1 / (1.0 + t1)
    return sel0, sel1, w0, w1


def _route_kernel(lg_ref, dest_ref, wts_ref, blk_ref, cnt_ref, *, tm, nblk_pad):
    phase = pl.program_id(0)
    i = pl.program_id(1)
    R = ROUTER_W

    @pl.when((phase == 0) & (i == 0))
    def _():
        cnt_ref[...] = jnp.zeros_like(cnt_ref)

    sel0, sel1, w0, w1 = _route_choice(lg_ref[...].T)
    picks = (sel0 | sel1).astype(F32)
    picked = jnp.sum(picks, axis=1, keepdims=True)

    @pl.when(phase == 0)
    def _():
        cnt_ref[0] = cnt_ref[0] + picked

    @pl.when(phase == 1)
    def _():
        dot = functools.partial(jnp.dot, preferred_element_type=F32)
        nb = jnp.floor((cnt_ref[0] + (EXPERT_ROWS - 1)) * (1.0 / EXPERT_ROWS))
        nb_hi = jnp.floor(nb * (1.0 / 32.0))
        nb_lo = nb - 32.0 * nb_hi
        r = lax.broadcasted_iota(I32, (R, R), 0)
        c = lax.broadcasted_iota(I32, (R, R), 1)
        lower = (c < r).astype(F32).astype(BF16)
        start_blk = 32.0 * dot(lower, nb_hi.astype(BF16)) + dot(lower, nb_lo.astype(BF16))

        @pl.when(i == 0)
        def _():
            cnt_ref[1] = jnp.zeros((R, LANES), F32)
            end_col = (start_blk + nb)[:, 0:1]
            erow = lax.broadcasted_iota(I32, (R, nblk_pad), 0)
            is_expert = (erow >= MOE_GROUPS) & (erow < MOE_GROUPS + MOE_EXPERTS)
            bidx = lax.broadcasted_iota(I32, (R, nblk_pad), 1).astype(F32)
            be = jnp.sum((is_expert & (end_col <= bidx)).astype(F32), axis=0, keepdims=True)
            be = jnp.minimum(be, float(MOE_EXPERTS - 1))
            n_used = jnp.sum(nb[:, 0:1], axis=0, keepdims=True)
            out_row = lax.broadcasted_iota(I32, blk_ref.shape, 0)
            blk_ref[...] = jnp.where(out_row == 0, be, n_used).astype(I32)

        tr = lax.broadcasted_iota(I32, (tm, tm), 0)
        tc = lax.broadcasted_iota(I32, (tm, tm), 1)
        earlier = (tr < tc).astype(F32).astype(BF16)
        before = dot(picks.astype(BF16), earlier)
        pos = (start_blk * float(EXPERT_ROWS) + cnt_ref[1])[:, 0:1] + before
        dest_ref[0:1, :] = jnp.sum(jnp.where(sel0, pos, 0.0), axis=0, keepdims=True).astype(I32)
        dest_ref[1:2, :] = jnp.sum(jnp.where(sel1, pos, 0.0), axis=0, keepdims=True).astype(I32)
        cnt_ref[1] = cnt_ref[1] + picked
        wrow = lax.broadcasted_iota(I32, (R, tm), 0)
        wts_ref[...] = jnp.where(wrow == 0, w0, jnp.where(wrow == 1, w1, 0.0)).T


def _route(logits, tm, nblk_pad):
    n = logits.shape[0]
    return pl.pallas_call(
        functools.partial(_route_kernel, tm=tm, nblk_pad=nblk_pad),
        grid=(2, n // tm),
        in_specs=[pl.BlockSpec((tm, ROUTER_W), lambda p, i: (i, 0))],
        out_specs=[
            pl.BlockSpec((2, tm), lambda p, i: (0, i * p)),
            pl.BlockSpec((tm, LANES), lambda p, i: (i * p, 0)),
            pl.BlockSpec((8, nblk_pad), lambda p, i: (0, 0)),
        ],
        out_shape=[
            jax.ShapeDtypeStruct((2, n), I32),
            jax.ShapeDtypeStruct((n, LANES), F32),
            jax.ShapeDtypeStruct((8, nblk_pad), I32),
        ],
        scratch_shapes=[pltpu.VMEM((2, ROUTER_W, LANES), F32)],
        compiler_params=_cparams("arbitrary", "arbitrary"),
        name="route",
    )(logits)


ISSUE_UNROLL = 8
COMBINE_SUBTILES = 2
MIX_SUBTILES = 1


def _dispatch_kernel(dest_ref, blk_ref, nused_ref, xn_ref, xs_ref, zbuf, sem, zsem, *, tm, nblk):
    i = pl.program_id(0)
    base = i * tm
    block_rows = EXPERT_ROWS * ROW_TILE

    @pl.when(i == 0)
    def _():
        zbuf[...] = jnp.zeros_like(zbuf)
        last_used = nused_ref[0] - 1

        def needs_clear(j):
            return (j >= last_used) | (blk_ref[j] != blk_ref[jnp.minimum(j + 1, nblk - 1)])

        def clear(j):
            rows = pl.ds(pl.multiple_of(j * block_rows, block_rows), block_rows)
            return pltpu.make_async_copy(zbuf, xs_ref.at[rows, :], zsem)

        def start(j, c):
            @pl.when(needs_clear(j))
            def _():
                clear(j).start()
            return c

        def wait(j, c):
            @pl.when(needs_clear(j))
            def _():
                clear(j).wait()
            return c

        lax.fori_loop(0, nblk, start, 0)
        lax.fori_loop(0, nblk, wait, 0)

    n_tokens = tm * pl.num_programs(0)

    def issue(t, c):
        for k in range(2):
            pltpu.make_async_copy(_token_row(xn_ref, t),
                                  _token_row(xs_ref, dest_ref[k * n_tokens + base + t]), sem).start(priority=k)
        return c

    lax.fori_loop(0, tm, issue, 0, unroll=ISSUE_UNROLL)
    for k in range(2):
        pltpu.make_async_copy(xn_ref, xs_ref.at[pl.ds(0, tm * ROW_TILE), :], sem).wait()


def _dispatch(dest_flat, blk_e, n_used, xn_packed, cap, tm):
    n = xn_packed.shape[0] // ROW_TILE
    nblk = cap // EXPERT_ROWS
    return pl.pallas_call(
        functools.partial(_dispatch_kernel, tm=tm, nblk=nblk),
        grid_spec=pltpu.PrefetchScalarGridSpec(
            num_scalar_prefetch=3,
            grid=(n // tm,),
            in_specs=[pl.BlockSpec((tm * ROW_TILE, LANES), lambda i, d, b, u: (i, 0))],
            out_specs=pl.BlockSpec(memory_space=pl.ANY),
            scratch_shapes=[pltpu.VMEM((EXPERT_ROWS * ROW_TILE, LANES), U32),
                            pltpu.SemaphoreType.DMA, pltpu.SemaphoreType.DMA],
        ),
        out_shape=jax.ShapeDtypeStruct((cap * ROW_TILE, LANES), U32),
        compiler_params=_cparams("arbitrary"),
        name="dispatch",
    )(dest_flat, blk_e, n_used, xn_packed)


def _expert_kernel(blk_ref, nused_ref, xs_ref, wg_hbm, wu_hbm, wd_hbm, y_ref,
                   wg_f, wu_f, wd_f, wg_b, wu_b, wd_b, slot_ref, sems, *, nblk):
    i = pl.program_id(0)
    nused = nused_ref[0]
    active = i < nused
    expert = blk_ref[i]
    new_expert = (i == 0) | (expert != blk_ref[jnp.maximum(i - 1, 0)])

    def fetch(e, slot):
        return [pltpu.make_async_copy(w_hbm.at[e], w_f.at[slot], sems.at[slot])
                for w_hbm, w_f in ((wg_hbm, wg_f), (wu_hbm, wu_f), (wd_hbm, wd_f))]

    @pl.when(jnp.logical_not(active))
    def _():
        y_ref[...] = jnp.zeros_like(y_ref)

    @pl.when(i == 0)
    def _():
        slot_ref[0] = 0
        for c in fetch(expert, 0):
            c.start()

    @pl.when(active & new_expert)
    def _():
        slot = slot_ref[0]
        for c in fetch(expert, slot):
            c.wait()
        wg_b[...] = wg_f[slot].astype(BF16)
        wu_b[...] = wu_f[slot].astype(BF16)
        wd_b[...] = wd_f[slot].astype(BF16)
        nxt = lax.while_loop(lambda j: (j < nused) & (blk_ref[jnp.minimum(j, nblk - 1)] == expert),
                             lambda j: j + 1, i + 1)

        @pl.when(nxt < nused)
        def _():
            for c in fetch(blk_ref[jnp.minimum(nxt, nblk - 1)], 1 - slot):
                c.start()

        slot_ref[0] = 1 - slot

    @pl.when(active)
    def _():
        dot = functools.partial(jnp.dot, preferred_element_type=F32)
        lo, hi = _unpack_bf16_pair(_load_token_rows(xs_ref, EXPERT_ROWS))
        lo = lo.astype(BF16)
        hi = hi.astype(BF16)
        g = dot(lo, wg_b[:HALF, :]) + dot(hi, wg_b[HALF:, :])
        u = dot(lo, wu_b[:HALF, :]) + dot(hi, wu_b[HALF:, :])
        hid = (_silu(g) * u).astype(BF16)
        y = dot(hid, wd_b[...])
        _store_token_rows(y_ref, _pack_bf16_pair(y[:, :HALF], y[:, HALF:]))


def _experts(blk_e, n_used, xs, w_gate, w_up, w_down):
    cap = xs.shape[0] // ROW_TILE
    nblk = cap // EXPERT_ROWS
    row = lambda i, blk, nu: (jnp.minimum(i, nu[0] - 1), 0)
    hbm = pl.BlockSpec(memory_space=pl.ANY)
    up_shape, down_shape = (D_MODEL, MOE_HIDDEN), (MOE_HIDDEN, D_MODEL)
    return pl.pallas_call(
        functools.partial(_expert_kernel, nblk=nblk),
        grid_spec=pltpu.PrefetchScalarGridSpec(
            num_scalar_prefetch=2,
            grid=(nblk,),
            in_specs=[pl.BlockSpec((EXPERT_ROWS * ROW_TILE, LANES), row), hbm, hbm, hbm],
            out_specs=pl.BlockSpec((EXPERT_ROWS * ROW_TILE, LANES), lambda i, blk, nu: (i, 0)),
            scratch_shapes=[
                pltpu.VMEM((2,) + up_shape, F32), pltpu.VMEM((2,) + up_shape, F32),
                pltpu.VMEM((2,) + down_shape, F32),
                pltpu.VMEM(up_shape, BF16), pltpu.VMEM(up_shape, BF16), pltpu.VMEM(down_shape, BF16),
                pltpu.SMEM((1,), I32), pltpu.SemaphoreType.DMA((2,)),
            ],
        ),
        out_shape=jax.ShapeDtypeStruct((cap * ROW_TILE, LANES), U32),
        compiler_params=_cparams("arbitrary"),
        name="experts",
    )(blk_e, n_used, xs, w_gate, w_up, w_down)


def _combine_kernel(dest_ref, h_ref, wts_ref, p_ref, y_ref, wpi_ref, gp_ref, wpg_ref, gfin_ref, o_ref,
                    ybuf_even, ybuf_odd, sems, *, tm):
    i = pl.program_id(0)
    last = pl.num_programs(0) - 1
    n_tokens = tm * pl.num_programs(0)
    dot = functools.partial(jnp.dot, preferred_element_type=F32)

    def row_copy(tile, t, k, buf, sem):
        return pltpu.make_async_copy(_token_row(y_ref, dest_ref[k * n_tokens + tile * tm + t]),
                                     _token_row(buf.at[k], t), sem)

    def wait_tile(buf, sem):
        for k in range(2):
            pltpu.make_async_copy(y_ref.at[pl.ds(0, tm * ROW_TILE), :], buf.at[k], sem).wait()

    @pl.when(i == 0)
    def _():
        def issue(t, c):
            for k in range(2):
                row_copy(0, t, k, ybuf_even, sems.at[0]).start(priority=1)
            return c

        lax.fori_loop(0, tm, issue, 0, unroll=ISSUE_UNROLL)

    def step(cur, cur_sem, nxt, nxt_sem):
        wait_tile(cur, cur_sem)
        nxt_tile = jnp.minimum(i + 1, last)
        for t in range(tm):
            for k in range(2):
                row_copy(nxt_tile, t, k, nxt, nxt_sem).start(priority=1)

        sub = tm // COMBINE_SUBTILES
        for r in range(COMBINE_SUBTILES):
            rows = slice(r * sub, (r + 1) * sub)
            ple = _rms(dot(p_ref[rows, :].astype(BF16), wpi_ref[...])) * gp_ref[...]
            w = wts_ref[rows, :]
            w0 = w[:, 0:1]
            w1 = w[:, 1:2]
            lo0, hi0 = _unpack_bf16_pair(_load_token_rows(cur.at[0], sub, first=r * sub))
            lo1, hi1 = _unpack_bf16_pair(_load_token_rows(cur.at[1], sub, first=r * sub))
            moe = jnp.concatenate([w0 * lo0 + w1 * lo1, w0 * hi0 + w1 * hi1], axis=-1)
            h = h_ref[rows, :] + moe
            gate = _sigmoid(dot(_rms(h).astype(BF16), wpg_ref[...]))
            h = h + ple * gate
            o_ref[rows, :] = _rms(h) * gfin_ref[...]

        @pl.when(i == last)
        def _():
            wait_tile(nxt, nxt_sem)

    even = lax.rem(i, 2) == 0

    @pl.when(even)
    def _():
        step(ybuf_even, sems.at[0], ybuf_odd, sems.at[1])

    @pl.when(jnp.logical_not(even))
    def _():
        step(ybuf_odd, sems.at[1], ybuf_even, sems.at[0])


def _combine_ple(dest_flat, h1, wts, p2, y, w_ple_in, g_ple, w_ple_gate, g_final, tm):
    n = h1.shape[0]
    const = lambda shape: pl.BlockSpec(shape, lambda i, d: (0,) * len(shape))
    return pl.pallas_call(
        functools.partial(_combine_kernel, tm=tm),
        grid_spec=pltpu.PrefetchScalarGridSpec(
            num_scalar_prefetch=1,
            grid=(n // tm,),
            in_specs=[
                pl.BlockSpec((tm, D_MODEL), lambda i, d: (i, 0)),
                pl.BlockSpec((tm, LANES), lambda i, d: (i, 0)),
                pl.BlockSpec((tm, PLE_DIM), lambda i, d: (i, 0)),
                pl.BlockSpec(memory_space=pl.ANY),
                const((PLE_DIM, D_MODEL)),
                const((1, D_MODEL)),
                const((D_MODEL, D_MODEL)),
                const((1, D_MODEL)),
            ],
            out_specs=pl.BlockSpec((tm, D_MODEL), lambda i, d: (i, 0)),
            scratch_shapes=[pltpu.VMEM((2, tm * ROW_TILE, LANES), U32), pltpu.VMEM((2, tm * ROW_TILE, LANES), U32),
                            pltpu.SemaphoreType.DMA((2,))],
        ),
        out_shape=jax.ShapeDtypeStruct((n, D_MODEL), F32),
        compiler_params=_cparams("arbitrary"),
        name="combine_ple",
    )(dest_flat, h1, wts, p2, y, w_ple_in, g_ple, w_ple_gate, g_final)


def _tile(n, pref):
    t = min(n, pref)
    assert n % t == 0, (n, t)
    return t


def kernel(x, p, g_mix, w_in, b_qkv, sinks, rel_bias, lb_logits, g_hgrn, w_up_att, w_up_rec, w_out, g_ffn,
           w_coarse, b_coarse, w_fine, b_fine, w_gate, w_up, w_down, w_ple_in, g_ple, w_ple_gate, g_final):
    B, T, D = x.shape
    assert D == D_MODEL and T % ATT_BLOCK == 0 and lb_logits.shape[0] == 2
    n = B * T
    layer = 0
    x2 = x.reshape(n, D)
    row = lambda a: a.reshape(1, -1)

    w_in_b = w_in[layer].astype(BF16)
    qkv = _qkv_proj(x2, row(g_mix[layer]), w_in_b[:, :ATT_QKV_W], row(b_qkv[layer]), _tile(n, 512))
    hz, f_raw = _hz_proj(x2, row(g_mix[layer]), w_in_b[:, ATT_QKV_W:], _tile(n, 1024))

    o_att = _swa(qkv, _bias_table(rel_bias), row(sinks[layer]), B, T)
    o_rec = _hgrn(hz, f_raw, lb_logits, row(g_hgrn[layer]), B, T, _tile(T, 512))

    w_router = jnp.concatenate(
        [w_coarse[layer], w_fine[layer], jnp.zeros((D, ROUTER_W - MOE_GROUPS - MOE_EXPERTS), F32)], axis=1)
    b_router = jnp.concatenate(
        [b_coarse[layer], b_fine[layer], jnp.zeros((ROUTER_W - MOE_GROUPS - MOE_EXPERTS,), F32)]).reshape(1, -1)
    h1, xn_packed, logits = _mix_out(
        o_att, o_rec, hz, x2, w_up_att[layer].astype(BF16), w_up_rec[layer].astype(BF16),
        w_out[layer].astype(BF16), row(g_ffn[layer]), w_router, b_router, _tile(n, 256))

    cap = 2 * n + MOE_EXPERTS * EXPERT_ROWS
    nblk = cap // EXPERT_ROWS
    nblk_pad = (nblk + LANES - 1) // LANES * LANES
    dest, wts, blk_tab = _route(logits, _tile(n, 512), nblk_pad)
    dest_flat = dest.reshape(2 * n)
    blk_e = blk_tab[0, :nblk]
    n_used = blk_tab[1, 0:1]

    xs = _dispatch(dest_flat, blk_e, n_used, xn_packed, cap, _tile(n, 512))
    y = _experts(blk_e, n_used, xs, w_gate[layer], w_up[layer], w_down[layer])
    out = _combine_ple(dest_flat, h1, wts, p[layer].reshape(n, PLE_DIM), y, w_ple_in[layer].astype(BF16),
                       row(g_ple[layer]), w_ple_gate[layer].astype(BF16), row(g_final), _tile(n, 256))
    return out.reshape(B, T, D)
```

```python
import functools
import math

import numpy as np
import jax
import jax.numpy as jnp
from jax import lax
from jax.experimental import pallas as pl
from jax.experimental.pallas import tpu as pltpu

F32 = jnp.float32
BF16 = jnp.bfloat16
I32 = jnp.int32
U32 = jnp.uint32

D_MODEL = 2048
ATT_Q_HEADS = 16
ATT_KV_HEADS = 2
ATT_GROUP = ATT_Q_HEADS // ATT_KV_HEADS
ATT_HEAD_DIM = 64
WINDOW = 128
ATT_BLOCK = 128
ATT_Q_W = ATT_Q_HEADS * ATT_HEAD_DIM
ATT_KV_W = ATT_KV_HEADS * ATT_HEAD_DIM
ATT_QKV_W = ATT_Q_W + 2 * ATT_KV_W
REL_BUCKETS = 32
REL_MAX_DIST = 128
HGRN_HEADS = 8
HGRN_DK = 128
HGRN_DV = 128
HGRN_W = HGRN_HEADS * HGRN_DV
HGRN_CHUNK = 64
MOE_GROUPS = 8
MOE_EXPERTS_PER_GROUP = 8
MOE_EXPERTS = MOE_GROUPS * MOE_EXPERTS_PER_GROUP
MOE_HIDDEN = 512
PLE_DIM = 256
EPS = 1e-6
LOG2_E = math.log2(math.e)

HALF = D_MODEL // 2
ROUTER_W = 128
EXPERT_ROWS = 256
VMEM_LIMIT = 56 * 1024 * 1024


def _cparams(*sem):
    return pltpu.CompilerParams(dimension_semantics=sem, vmem_limit_bytes=VMEM_LIMIT)


def _rms(xf):
    return xf * lax.rsqrt(jnp.mean(xf * xf, axis=-1, keepdims=True) + EPS)


def _sigmoid(x):
    return 0.5 + 0.5 * jnp.tanh(0.5 * x)


def _silu(x):
    h = 0.5 * x
    return h + h * jnp.tanh(h)


def _pack_bf16_pair(lo_f32, hi_f32):
    lo = lax.bitcast_convert_type(lo_f32.astype(BF16).astype(F32), U32)
    hi = lax.bitcast_convert_type(hi_f32.astype(BF16).astype(F32), U32)
    return hi | (lo >> 16)


def _unpack_bf16_pair(w):
    lo = lax.bitcast_convert_type(w << 16, F32)
    hi = lax.bitcast_convert_type(w & jnp.uint32(0xFFFF0000), F32)
    return lo, hi


LANES = 128
ROW_TILE = HALF // LANES


def _store_token_rows(ref, words, first=0):
    tokens = words.shape[0]
    for c in range(ROW_TILE):
        ref[pl.ds(first * ROW_TILE + c, tokens, stride=ROW_TILE), :] = words[:, c * LANES:(c + 1) * LANES]


def _load_token_rows(ref, tokens, first=0):
    return jnp.concatenate([ref[pl.ds(first * ROW_TILE + c, tokens, stride=ROW_TILE), :]
                            for c in range(ROW_TILE)], axis=-1)


def _token_row(ref, idx):
    return ref.at[pl.ds(pl.multiple_of(idx * ROW_TILE, ROW_TILE), ROW_TILE), :]


def _qkv_kernel(x_ref, g_ref, w_ref, b_ref, o_ref):
    u = (_rms(x_ref[...]) * g_ref[...]).astype(BF16)
    acc = jnp.dot(u, w_ref[...], preferred_element_type=F32)
    o_ref[...] = (acc + b_ref[...]).astype(o_ref.dtype)


def _qkv_proj(x2, g_mix, w_qkv, b_qkv, tm):
    n = x2.shape[0]
    return pl.pallas_call(
        _qkv_kernel,
        grid=(n // tm,),
        in_specs=[
            pl.BlockSpec((tm, D_MODEL), lambda i: (i, 0)),
            pl.BlockSpec((1, D_MODEL), lambda i: (0, 0)),
            pl.BlockSpec((D_MODEL, ATT_QKV_W), lambda i: (0, 0)),
            pl.BlockSpec((1, ATT_QKV_W), lambda i: (0, 0)),
        ],
        out_specs=pl.BlockSpec((tm, ATT_QKV_W), lambda i: (i, 0)),
        out_shape=jax.ShapeDtypeStruct((n, ATT_QKV_W), BF16),
        compiler_params=_cparams("parallel"),
        name="qkv_proj",
    )(x2, g_mix, w_qkv, b_qkv)


HZ_TN = 1024
HZ_COLS = 4 * HGRN_W + 2 * D_MODEL
HZ_F_BLOCK = 1


def _hz_kernel(x_ref, g_ref, w_ref, hz_ref, f_ref, u_scr):
    j = pl.program_id(1)

    @pl.when(j == 0)
    def _():
        u_scr[...] = (_rms(x_ref[...]) * g_ref[...]).astype(BF16)

    acc = jnp.dot(u_scr[...], w_ref[...], preferred_element_type=F32)
    hz_ref[...] = acc.astype(BF16)

    @pl.when(j == HZ_F_BLOCK)
    def _():
        f_ref[...] = acc


def _hz_proj(x2, g_mix, w_hz, tm):
    n = x2.shape[0]
    return pl.pallas_call(
        _hz_kernel,
        grid=(n // tm, HZ_COLS // HZ_TN),
        in_specs=[
            pl.BlockSpec((tm, D_MODEL), lambda i, j: (i, 0)),
            pl.BlockSpec((1, D_MODEL), lambda i, j: (0, 0)),
            pl.BlockSpec((D_MODEL, HZ_TN), lambda i, j: (0, j)),
        ],
        out_specs=[
            pl.BlockSpec((tm, HZ_TN), lambda i, j: (i, j)),
            pl.BlockSpec((tm, HGRN_W), lambda i, j: (i, 0)),
        ],
        out_shape=[
            jax.ShapeDtypeStruct((n, HZ_COLS), BF16),
            jax.ShapeDtypeStruct((n, HGRN_W), F32),
        ],
        scratch_shapes=[pltpu.VMEM((tm, D_MODEL), BF16)],
        compiler_params=_cparams("parallel", "arbitrary"),
        name="hz_proj",
    )(x2, g_mix, w_hz)


def _t5_bucket_table():
    qi = np.arange(ATT_BLOCK)[:, None]
    kj = np.arange(2 * ATT_BLOCK)[None, :]
    dist = qi + ATT_BLOCK - kj
    exact = REL_BUCKETS // 2
    d = np.maximum(dist, 0)
    large = exact + (np.log(np.maximum(d, 1).astype(np.float32) / exact)
                     / math.log(REL_MAX_DIST / exact) * (REL_BUCKETS - exact)).astype(np.int32)
    large = np.minimum(large, REL_BUCKETS - 1)
    return np.where(d < exact, d, large).astype(np.int32)


def _bias_kernel(bucket_ref, rel_ref, o_ref):
    bk = bucket_ref[...]
    qi = lax.broadcasted_iota(I32, bk.shape, 0)
    kj = lax.broadcasted_iota(I32, bk.shape, 1)
    dist = qi + ATT_BLOCK - kj
    in_window = (dist >= 0) & (dist < WINDOW)
    first_block = in_window & (kj >= ATT_BLOCK)
    neg = jnp.float32(-jnp.inf)
    for h in range(ATT_Q_HEADS):
        acc = jnp.zeros(bk.shape, F32)
        for b in range(REL_BUCKETS):
            acc = jnp.where(bk == b, rel_ref[b, h], acc)
        o_ref[0, h] = jnp.where(first_block, acc, neg)
        o_ref[1, h] = jnp.where(in_window, acc, neg)


def _bias_table(rel_bias):
    bucket = jnp.asarray(_t5_bucket_table())
    return pl.pallas_call(
        _bias_kernel,
        in_specs=[
            pl.BlockSpec(memory_space=pltpu.VMEM),
            pl.BlockSpec(memory_space=pltpu.SMEM),
        ],
        out_specs=pl.BlockSpec(memory_space=pltpu.VMEM),
        out_shape=jax.ShapeDtypeStruct((2, ATT_Q_HEADS, ATT_BLOCK, 2 * ATT_BLOCK), F32),
        name="bias_table",
    )(bucket, rel_bias)


SWA_BLOCKS = 2


def _swa_kernel(q_ref, kvc_ref, kvp_ref, bias_ref, sink_ref, o_ref):
    hd = ATT_HEAD_DIM
    blk = ATT_BLOCK
    nt = (((1,), (1,)), ((), ()))
    step = pl.program_id(1)
    left = lax.broadcasted_iota(I32, (2 * blk, 2 * hd), 1) < hd
    left_q = lax.broadcasted_iota(I32, (blk, 2 * hd), 1) < hd
    one = jnp.ones((), F32)
    zero = jnp.zeros((), F32)
    for sb in range(SWA_BLOCKS):
        rows = slice(sb * blk, (sb + 1) * blk)
        prev = kvp_ref[...] if sb == 0 else kvc_ref[(sb - 1) * blk:sb * blk, :]
        band = jnp.concatenate([prev, kvc_ref[rows, :]], axis=0).astype(F32)
        kcol = band[:, :2 * hd] * (hd ** -0.5)
        vcol = band[:, 2 * hd:]
        krot = pltpu.roll(kcol, hd, axis=1)
        vrot = pltpu.roll(vcol, hd, axis=1)
        table = jnp.minimum(step * SWA_BLOCKS + sb, 1) if sb == 0 else 1
        for kk in range(ATT_KV_HEADS):
            k_src, k_alt = (kcol, krot) if kk == 0 else (krot, kcol)
            v_src, v_alt = (vcol, vrot) if kk == 0 else (vrot, vcol)
            k_a = jnp.where(left, k_src, zero).astype(BF16)
            k_b = jnp.where(left, zero, k_alt).astype(BF16)
            v_a = jnp.where(left, v_src, one).astype(BF16)
            v_b = jnp.where(left, one, v_alt).astype(BF16)
            pairs = range(kk * ATT_GROUP // 2, (kk + 1) * ATT_GROUP // 2)
            scores = []
            for j in pairs:
                qp = q_ref[rows, 2 * hd * j:2 * hd * (j + 1)]
                scores.append((lax.dot_general(qp, k_a, nt, preferred_element_type=F32),
                               lax.dot_general(qp, k_b, nt, preferred_element_type=F32)))
            probs, sink_terms = [], []
            for j, pair in zip(pairs, scores):
                p_pair, t_pair = [], []
                for h, s in zip((2 * j, 2 * j + 1), pair):
                    s = s + bias_ref[table, h]
                    sink = sink_ref[0, h]
                    m = jnp.maximum(jnp.max(s, axis=-1, keepdims=True), sink)
                    p_pair.append(jnp.exp(s - m).astype(BF16))
                    t_pair.append(jnp.exp(sink - m))
                probs.append(p_pair)
                sink_terms.append(t_pair)
            ext = [(jnp.dot(pa, v_a, preferred_element_type=F32), jnp.dot(pb, v_b, preferred_element_type=F32))
                   for pa, pb in probs]
            for j, (ea, eb), (ta, tb) in zip(pairs, ext, sink_terms):
                num = jnp.where(left_q, ea, eb)
                den = pltpu.roll(jnp.where(left_q, eb, ea), hd, axis=1)
                den = den + jnp.where(left_q, ta, tb)
                o_ref[rows, 2 * hd * j:2 * hd * (j + 1)] = (num / den).astype(o_ref.dtype)


def _swa(qkv, bias_tab, sinks, batch, seq):
    n = qkv.shape[0]
    nb = seq // ATT_BLOCK
    assert nb % SWA_BLOCKS == 0
    ns = nb // SWA_BLOCKS
    rows = SWA_BLOCKS * ATT_BLOCK
    kv_col = ATT_Q_W // (2 * ATT_KV_W)

    return pl.pallas_call(
        _swa_kernel,
        grid=(batch, ns),
        in_specs=[
            pl.BlockSpec((rows, ATT_Q_W), lambda b, i: (b * ns + i, 0)),
            pl.BlockSpec((rows, 2 * ATT_KV_W), lambda b, i: (b * ns + i, kv_col)),
            pl.BlockSpec((ATT_BLOCK, 2 * ATT_KV_W),
                         lambda b, i: (b * nb + jnp.maximum(i * SWA_BLOCKS - 1, 0), kv_col)),
            pl.BlockSpec((2, ATT_Q_HEADS, ATT_BLOCK, 2 * ATT_BLOCK), lambda b, i: (0, 0, 0, 0)),
            pl.BlockSpec(memory_space=pltpu.SMEM),
        ],
        out_specs=pl.BlockSpec((rows, ATT_Q_W), lambda b, i: (b * ns + i, 0)),
        out_shape=jax.ShapeDtypeStruct((n, ATT_Q_W), BF16),
        compiler_params=_cparams("parallel", "arbitrary"),
        name="swa",
    )(qkv, qkv, qkv, bias_tab, sinks)


def _cumsum_rows(tri_bf16, g):
    w = g.shape[1]
    g1 = g.astype(BF16)
    r1 = g - g1.astype(F32)
    g2 = r1.astype(BF16)
    g3 = (r1 - g2.astype(F32)).astype(BF16)
    parts = jnp.dot(tri_bf16, jnp.concatenate([g1, g2, g3], axis=1), preferred_element_type=F32)
    return parts[:, :w] + (parts[:, w:2 * w] + parts[:, 2 * w:])


def _hgrn_kernel(q_ref, f_ref, i_ref, g_ref, lbl_ref, gh_ref, o_ref, st_ref, b_scr, k_scr, *, chunks):
    C = HGRN_CHUNK
    ref_row = C // 2 - 1

    @pl.when(pl.program_id(1) == 0)
    def _():
        st_ref[...] = jnp.zeros_like(st_ref)

    l = lbl_ref[...]
    e = jnp.exp(l - jnp.max(l, axis=0, keepdims=True))
    lb_all = e[0:1, :] / jnp.sum(e, axis=0, keepdims=True)
    f_mid = 0.5 * (1.0 + lb_all)
    f_half = 0.5 * (1.0 - lb_all)

    row = lax.broadcasted_iota(I32, (C, C), 0)
    col = lax.broadcasted_iota(I32, (C, C), 1)
    causal = row >= col
    tri = causal.astype(F32).astype(BF16)
    gain = gh_ref[...]

    def chunk_rows(c):
        return pl.ds(pl.multiple_of(c * C, C), C)

    def decays(c, slot):
        swing = f_half * jnp.tanh(0.5 * f_ref[chunk_rows(c), :])
        k_scr[slot] = f_half - swing
        b_scr[slot] = _cumsum_rows(tri, jnp.log(f_mid + swing)) * LOG2_E

    def outputs(c, slot):
        rows = chunk_rows(c)
        b = b_scr[slot]
        k = k_scr[slot]
        qs = _silu(q_ref[rows, :].astype(F32))
        b_ref = b[ref_row:ref_row + 1, :]
        b_last = b[C - 1:C, :]
        qe_all = (qs * jnp.exp2(b)).astype(BF16)
        qa_all = (qs * jnp.exp2(b - b_ref)).astype(BF16)
        ka_all = (k * jnp.exp2(b_ref - b)).astype(BF16)
        kl_all = (k * jnp.exp2(b_last - b)).astype(BF16)
        decay = jnp.exp2(b_last)
        gate = _silu(g_ref[rows, :].astype(F32))
        nt = (((1,), (1,)), ((), ()))
        heads = [slice(HGRN_DK * h, HGRN_DK * (h + 1)) for h in range(HGRN_HEADS)]
        vs = [i_ref[rows, cs] for cs in heads]
        sts = [st_ref[h] for h in range(HGRN_HEADS)]
        attn = [lax.dot_general(qa_all[:, cs], ka_all[:, cs], nt, preferred_element_type=F32) for cs in heads]
        inter = [lax.dot_general(qe_all[:, cs], st.astype(BF16), nt, preferred_element_type=F32)
                 for cs, st in zip(heads, sts)]
        upd = [lax.dot_general(v, kl_all[:, cs], (((0,), (0,)), ((), ())), preferred_element_type=F32)
               for cs, v in zip(heads, vs)]
        for h, cs in enumerate(heads):
            st_ref[h] = sts[h] * decay[:, cs] + upd[h]
        intra = [jnp.dot(jnp.where(causal, a, 0.0).astype(BF16), v, preferred_element_type=F32)
                 for a, v in zip(attn, vs)]
        for h, cs in enumerate(heads):
            o = inter[h] + intra[h]
            o_ref[rows, cs] = (_rms(o) * gain * gate[:, cs]).astype(o_ref.dtype)

    decays(0, 0)

    def pair(p, carry):
        c = 2 * p
        decays(c + 1, 1)
        outputs(c, 0)
        decays(jnp.minimum(c + 2, chunks - 1), 0)
        outputs(c + 1, 1)
        return carry

    lax.fori_loop(0, chunks // 2, pair, 0)


def _hgrn(hz, f_raw, lb_logits, g_hgrn, batch, seq, tc):
    n = hz.shape[0]
    nt = seq // tc
    blk = lambda col: pl.BlockSpec((tc, HGRN_W), lambda b, t, col=col: (b * nt + t, col))
    return pl.pallas_call(
        functools.partial(_hgrn_kernel, chunks=tc // HGRN_CHUNK),
        grid=(batch, nt),
        in_specs=[
            blk(0),
            pl.BlockSpec((tc, HGRN_W), lambda b, t: (b * nt + t, 0)),
            blk(2),
            blk(3),
            pl.BlockSpec((2, HGRN_W), lambda b, t: (0, 0)),
            pl.BlockSpec((1, HGRN_DV), lambda b, t: (0, 0)),
        ],
        out_specs=pl.BlockSpec((tc, HGRN_W), lambda b, t: (b * nt + t, 0)),
        out_shape=jax.ShapeDtypeStruct((n, HGRN_W), BF16),
        scratch_shapes=[pltpu.VMEM((HGRN_HEADS, HGRN_DV, HGRN_DK), F32),
                        pltpu.VMEM((2, HGRN_CHUNK, HGRN_W), F32), pltpu.VMEM((2, HGRN_CHUNK, HGRN_W), F32)],
        compiler_params=_cparams("parallel", "arbitrary"),
        name="hgrn",
    )(hz, f_raw, hz, hz, lb_logits, g_hgrn)


def _split_hi_lo(a):
    hi = a.astype(BF16)
    lo = (a - hi.astype(F32)).astype(BF16)
    return hi, lo


def _mix_kernel(oa_ref, or_ref, ga_ref, gr_ref, x_ref, wa_ref, wr_ref, wo_ref, gf_ref, wrt_ref, brt_ref,
                h_ref, xn_ref, lg_ref):
    dot = functools.partial(jnp.dot, preferred_element_type=F32)
    w_hi, w_lo = _split_hi_lo(wrt_ref[...])
    tm = x_ref.shape[0]
    sub = tm // MIX_SUBTILES
    for r in range(MIX_SUBTILES):
        rows = slice(r * sub, (r + 1) * sub)
        ya = dot(oa_ref[rows, :], wa_ref[...])
        yr = dot(or_ref[rows, :], wr_ref[...])
        mix = _sigmoid(ga_ref[rows, :].astype(F32)) * ya + _sigmoid(gr_ref[rows, :].astype(F32)) * yr
        h = x_ref[rows, :] + dot(mix.astype(BF16), wo_ref[...])
        h_ref[rows, :] = h
        xn = _rms(h) * gf_ref[...]
        _store_token_rows(xn_ref, _pack_bf16_pair(xn[:, :HALF], xn[:, HALF:]), first=r * sub)
        x_hi, x_lo = _split_hi_lo(xn)
        lg_ref[rows, :] = dot(x_hi, w_hi) + (dot(x_hi, w_lo) + dot(x_lo, w_hi)) + brt_ref[...]


def _mix_out(o_att, o_rec, hz, x2, w_up_att, w_up_rec, w_out, g_ffn, w_router, b_router, tm):
    n = x2.shape[0]
    const = lambda shape: pl.BlockSpec(shape, lambda i: (0,) * len(shape))
    ga_col = 4 * HGRN_W // D_MODEL
    return pl.pallas_call(
        _mix_kernel,
        grid=(n // tm,),
        in_specs=[
            pl.BlockSpec((tm, ATT_Q_W), lambda i: (i, 0)),
            pl.BlockSpec((tm, HGRN_W), lambda i: (i, 0)),
            pl.BlockSpec((tm, D_MODEL), lambda i: (i, ga_col)),
            pl.BlockSpec((tm, D_MODEL), lambda i: (i, ga_col + 1)),
            pl.BlockSpec((tm, D_MODEL), lambda i: (i, 0)),
            const((ATT_Q_W, D_MODEL)),
            const((HGRN_W, D_MODEL)),
            const((D_MODEL, D_MODEL)),
            const((1, D_MODEL)),
            const((D_MODEL, ROUTER_W)),
            const((1, ROUTER_W)),
        ],
        out_specs=[
            pl.BlockSpec((tm, D_MODEL), lambda i: (i, 0)),
            pl.BlockSpec((tm * ROW_TILE, LANES), lambda i: (i, 0)),
            pl.BlockSpec((tm, ROUTER_W), lambda i: (i, 0)),
        ],
        out_shape=[
            jax.ShapeDtypeStruct((n, D_MODEL), F32),
            jax.ShapeDtypeStruct((n * ROW_TILE, LANES), U32),
            jax.ShapeDtypeStruct((n, ROUTER_W), F32),
        ],
        compiler_params=_cparams("parallel"),
        name="mix_out",
    )(o_att, o_rec, hz, hz, x2, w_up_att, w_up_rec, w_out, g_ffn, w_router, b_router)


def _route_choice(lgt):
    row = lax.broadcasted_iota(I32, lgt.shape, 0).astype(F32)
    neg = jnp.float32(-jnp.inf)
    far = float(ROUTER_W)
    lc = jnp.where(row < MOE_GROUPS, lgt, neg)
    cmax = jnp.max(lc, axis=0, keepdims=True)
    p_sel = 1.0 / jnp.sum(jnp.exp(lc - cmax), axis=0, keepdims=True)
    grp = jnp.min(jnp.where(lc == cmax, row, far), axis=0, keepdims=True)
    fine = row - MOE_GROUPS
    first = grp * MOE_EXPERTS_PER_GROUP
    lf = jnp.where((fine >= first) & (fine < first + MOE_EXPERTS_PER_GROUP), lgt, neg)
    v0 = jnp.max(lf, axis=0, keepdims=True)
    sel0 = fine == jnp.min(jnp.where(lf == v0, fine, far), axis=0, keepdims=True)
    lf1 = jnp.where(sel0, neg, lf)
    v1 = jnp.max(lf1, axis=0, keepdims=True)
    sel1 = fine == jnp.min(jnp.where(lf1 == v1, fine, far), axis=0, keepdims=True)
    t1 = jnp.exp(v1 - v0)
    w0 = p_sel / (1.0 + t1)
    w1 = p_sel * t1 / (1.0 + t1)
    return sel0, sel1, w0, w1


def _route_kernel(lg_ref, dest_ref, wts_ref, blk_ref, cnt_ref, *, tm, nblk_pad):
    phase = pl.program_id(0)
    i = pl.program_id(1)
    R = ROUTER_W

    @pl.when((phase == 0) & (i == 0))
    def _():
        cnt_ref[...] = jnp.zeros_like(cnt_ref)

    sel0, sel1, w0, w1 = _route_choice(lg_ref[...].T)
    picks = (sel0 | sel1).astype(F32)
    picked = jnp.sum(picks, axis=1, keepdims=True)

    @pl.when(phase == 0)
    def _():
        cnt_ref[0] = cnt_ref[0] + picked

    @pl.when(phase == 1)
    def _():
        dot = functools.partial(jnp.dot, preferred_element_type=F32)
        nb = jnp.floor((cnt_ref[0] + (EXPERT_ROWS - 1)) * (1.0 / EXPERT_ROWS))
        nb_hi = jnp.floor(nb * (1.0 / 32.0))
        nb_lo = nb - 32.0 * nb_hi
        r = lax.broadcasted_iota(I32, (R, R), 0)
        c = lax.broadcasted_iota(I32, (R, R), 1)
        lower = (c < r).astype(F32).astype(BF16)
        start_blk = 32.0 * dot(lower, nb_hi.astype(BF16)) + dot(lower, nb_lo.astype(BF16))

        @pl.when(i == 0)
        def _():
            cnt_ref[1] = jnp.zeros((R, LANES), F32)
            end_col = (start_blk + nb)[:, 0:1]
            erow = lax.broadcasted_iota(I32, (R, nblk_pad), 0)
            is_expert = (erow >= MOE_GROUPS) & (erow < MOE_GROUPS + MOE_EXPERTS)
            bidx = lax.broadcasted_iota(I32, (R, nblk_pad), 1).astype(F32)
            be = jnp.sum((is_expert & (end_col <= bidx)).astype(F32), axis=0, keepdims=True)
            be = jnp.minimum(be, float(MOE_EXPERTS - 1))
            n_used = jnp.sum(nb[:, 0:1], axis=0, keepdims=True)
            out_row = lax.broadcasted_iota(I32, blk_ref.shape, 0)
            blk_ref[...] = jnp.where(out_row == 0, be, n_used).astype(I32)

        tr = lax.broadcasted_iota(I32, (tm, tm), 0)
        tc = lax.broadcasted_iota(I32, (tm, tm), 1)
        earlier = (tr < tc).astype(F32).astype(BF16)
        before = dot(picks.astype(BF16), earlier)
        pos = (start_blk * float(EXPERT_ROWS) + cnt_ref[1])[:, 0:1] + before
        dest_ref[0:1, :] = jnp.sum(jnp.where(sel0, pos, 0.0), axis=0, keepdims=True).astype(I32)
        dest_ref[1:2, :] = jnp.sum(jnp.where(sel1, pos, 0.0), axis=0, keepdims=True).astype(I32)
        cnt_ref[1] = cnt_ref[1] + picked
        wrow = lax.broadcasted_iota(I32, (R, tm), 0)
        wts_ref[...] = jnp.where(wrow == 0, w0, jnp.where(wrow == 1, w1, 0.0)).T


def _route(logits, tm, nblk_pad):
    n = logits.shape[0]
    return pl.pallas_call(
        functools.partial(_route_kernel, tm=tm, nblk_pad=nblk_pad),
        grid=(2, n // tm),
        in_specs=[pl.BlockSpec((tm, ROUTER_W), lambda p, i: (i, 0))],
        out_specs=[
            pl.BlockSpec((2, tm), lambda p, i: (0, i * p)),
            pl.BlockSpec((tm, LANES), lambda p, i: (i * p, 0)),
            pl.BlockSpec((8, nblk_pad), lambda p, i: (0, 0)),
        ],
        out_shape=[
            jax.ShapeDtypeStruct((2, n), I32),
            jax.ShapeDtypeStruct((n, LANES), F32),
            jax.ShapeDtypeStruct((8, nblk_pad), I32),
        ],
        scratch_shapes=[pltpu.VMEM((2, ROUTER_W, LANES), F32)],
        compiler_params=_cparams("arbitrary", "arbitrary"),
        name="route",
    )(logits)


ISSUE_UNROLL = 8
COMBINE_SUBTILES = 2
COMBINE_LEAD = 2
MIX_SUBTILES = 1


def _dispatch_kernel(dest_ref, blk_ref, nused_ref, xn_ref, xs_ref, zbuf, sem, zsem, *, tm, nblk):
    i = pl.program_id(0)
    base = i * tm
    block_rows = EXPERT_ROWS * ROW_TILE

    @pl.when(i == 0)
    def _():
        zbuf[...] = jnp.zeros_like(zbuf)
        last_used = nused_ref[0] - 1

        def needs_clear(j):
            return (j >= last_used) | (blk_ref[j] != blk_ref[jnp.minimum(j + 1, nblk - 1)])

        def clear(j):
            rows = pl.ds(pl.multiple_of(j * block_rows, block_rows), block_rows)
            return pltpu.make_async_copy(zbuf, xs_ref.at[rows, :], zsem)

        def start(j, c):
            @pl.when(needs_clear(j))
            def _():
                clear(j).start()
            return c

        def wait(j, c):
            @pl.when(needs_clear(j))
            def _():
                clear(j).wait()
            return c

        lax.fori_loop(0, nblk, start, 0)
        lax.fori_loop(0, nblk, wait, 0)

    n_tokens = tm * pl.num_programs(0)

    def issue(t, c):
        for k in range(2):
            pltpu.make_async_copy(_token_row(xn_ref, t),
                                  _token_row(xs_ref, dest_ref[k * n_tokens + base + t]), sem).start(priority=k)
        return c

    lax.fori_loop(0, tm, issue, 0, unroll=ISSUE_UNROLL)
    for k in range(2):
        pltpu.make_async_copy(xn_ref, xs_ref.at[pl.ds(0, tm * ROW_TILE), :], sem).wait()


def _dispatch(dest_flat, blk_e, n_used, xn_packed, cap, tm):
    n = xn_packed.shape[0] // ROW_TILE
    nblk = cap // EXPERT_ROWS
    return pl.pallas_call(
        functools.partial(_dispatch_kernel, tm=tm, nblk=nblk),
        grid_spec=pltpu.PrefetchScalarGridSpec(
            num_scalar_prefetch=3,
            grid=(n // tm,),
            in_specs=[pl.BlockSpec((tm * ROW_TILE, LANES), lambda i, d, b, u: (i, 0))],
            out_specs=pl.BlockSpec(memory_space=pl.ANY),
            scratch_shapes=[pltpu.VMEM((EXPERT_ROWS * ROW_TILE, LANES), U32),
                            pltpu.SemaphoreType.DMA, pltpu.SemaphoreType.DMA],
        ),
        out_shape=jax.ShapeDtypeStruct((cap * ROW_TILE, LANES), U32),
        compiler_params=_cparams("arbitrary"),
        name="dispatch",
    )(dest_flat, blk_e, n_used, xn_packed)


def _expert_kernel(blk_ref, nused_ref, xs_ref, wg_hbm, wu_hbm, wd_hbm, y_ref,
                   wg_f, wu_f, wd_f, wg_b, wu_b, wd_b, slot_ref, sems, *, nblk):
    i = pl.program_id(0)
    nused = nused_ref[0]
    active = i < nused
    expert = blk_ref[i]
    new_expert = (i == 0) | (expert != blk_ref[jnp.maximum(i - 1, 0)])

    def fetch(e, slot):
        return [pltpu.make_async_copy(w_hbm.at[e], w_f.at[slot], sems.at[slot])
                for w_hbm, w_f in ((wg_hbm, wg_f), (wu_hbm, wu_f), (wd_hbm, wd_f))]

    @pl.when(jnp.logical_not(active))
    def _():
        y_ref[...] = jnp.zeros_like(y_ref)

    @pl.when(i == 0)
    def _():
        slot_ref[0] = 0
        for c in fetch(expert, 0):
            c.start()

    @pl.when(active & new_expert)
    def _():
        slot = slot_ref[0]
        for c in fetch(expert, slot):
            c.wait()
        wg_b[...] = wg_f[slot].astype(BF16)
        wu_b[...] = wu_f[slot].astype(BF16)
        wd_b[...] = wd_f[slot].astype(BF16)
        nxt = lax.while_loop(lambda j: (j < nused) & (blk_ref[jnp.minimum(j, nblk - 1)] == expert),
                             lambda j: j + 1, i + 1)

        @pl.when(nxt < nused)
        def _():
            for c in fetch(blk_ref[jnp.minimum(nxt, nblk - 1)], 1 - slot):
                c.start()

        slot_ref[0] = 1 - slot

    @pl.when(active)
    def _():
        dot = functools.partial(jnp.dot, preferred_element_type=F32)
        lo, hi = _unpack_bf16_pair(_load_token_rows(xs_ref, EXPERT_ROWS))
        lo = lo.astype(BF16)
        hi = hi.astype(BF16)
        g = dot(lo, wg_b[:HALF, :]) + dot(hi, wg_b[HALF:, :])
        u = dot(lo, wu_b[:HALF, :]) + dot(hi, wu_b[HALF:, :])
        hid = (_silu(g) * u).astype(BF16)
        y = dot(hid, wd_b[...])
        _store_token_rows(y_ref, _pack_bf16_pair(y[:, :HALF], y[:, HALF:]))


def _experts(blk_e, n_used, xs, w_gate, w_up, w_down):
    cap = xs.shape[0] // ROW_TILE
    nblk = cap // EXPERT_ROWS
    row = lambda i, blk, nu: (jnp.minimum(i, nu[0] - 1), 0)
    hbm = pl.BlockSpec(memory_space=pl.ANY)
    up_shape, down_shape = (D_MODEL, MOE_HIDDEN), (MOE_HIDDEN, D_MODEL)
    return pl.pallas_call(
        functools.partial(_expert_kernel, nblk=nblk),
        grid_spec=pltpu.PrefetchScalarGridSpec(
            num_scalar_prefetch=2,
            grid=(nblk,),
            in_specs=[pl.BlockSpec((EXPERT_ROWS * ROW_TILE, LANES), row), hbm, hbm, hbm],
            out_specs=pl.BlockSpec((EXPERT_ROWS * ROW_TILE, LANES), lambda i, blk, nu: (i, 0)),
            scratch_shapes=[
                pltpu.VMEM((2,) + up_shape, F32), pltpu.VMEM((2,) + up_shape, F32),
                pltpu.VMEM((2,) + down_shape, F32),
                pltpu.VMEM(up_shape, BF16), pltpu.VMEM(up_shape, BF16), pltpu.VMEM(down_shape, BF16),
                pltpu.SMEM((1,), I32), pltpu.SemaphoreType.DMA((2,)),
            ],
        ),
        out_shape=jax.ShapeDtypeStruct((cap * ROW_TILE, LANES), U32),
        compiler_params=_cparams("arbitrary"),
        name="experts",
    )(blk_e, n_used, xs, w_gate, w_up, w_down)


def _combine_kernel(dest_ref, h_ref, wts_ref, p_ref, y_ref, wpi_ref, gp_ref, wpg_ref, gfin_ref, o_ref,
                    *scratch, tm):
    *bufs, sems = scratch
    i = pl.program_id(0)
    last = pl.num_programs(0) - 1
    n_tokens = tm * pl.num_programs(0)
    dot = functools.partial(jnp.dot, preferred_element_type=F32)

    def row_copy(tile, t, k, buf, sem):
        return pltpu.make_async_copy(_token_row(y_ref, dest_ref[k * n_tokens + tile * tm + t]),
                                     _token_row(buf.at[k], t), sem)

    def wait_tile(buf, sem):
        for k in range(2):
            pltpu.make_async_copy(y_ref.at[pl.ds(0, tm * ROW_TILE), :], buf.at[k], sem).wait()

    @pl.when(i == 0)
    def _():
        def issue(t, c):
            for tile in range(COMBINE_LEAD):
                for k in range(2):
                    row_copy(jnp.minimum(tile, last), t, k, bufs[tile], sems.at[tile]).start(priority=1)
            return c

        lax.fori_loop(0, tm, issue, 0, unroll=ISSUE_UNROLL)

    def step(r):
        cur, cur_sem = bufs[r], sems.at[r]
        ahead = (r + COMBINE_LEAD) % len(bufs)
        nxt, nxt_sem = bufs[ahead], sems.at[ahead]
        wait_tile(cur, cur_sem)
        nxt_tile = jnp.minimum(i + COMBINE_LEAD, last)
        for t in range(tm):
            for k in range(2):
                row_copy(nxt_tile, t, k, nxt, nxt_sem).start(priority=1)

        sub = tm // COMBINE_SUBTILES
        for r in range(COMBINE_SUBTILES):
            rows = slice(r * sub, (r + 1) * sub)
            ple = _rms(dot(p_ref[rows, :].astype(BF16), wpi_ref[...])) * gp_ref[...]
            w = wts_ref[rows, :]
            w0 = w[:, 0:1]
            w1 = w[:, 1:2]
            lo0, hi0 = _unpack_bf16_pair(_load_token_rows(cur.at[0], sub, first=r * sub))
            lo1, hi1 = _unpack_bf16_pair(_load_token_rows(cur.at[1], sub, first=r * sub))
            moe = jnp.concatenate([w0 * lo0 + w1 * lo1, w0 * hi0 + w1 * hi1], axis=-1)
            h = h_ref[rows, :] + moe
            gate = _sigmoid(dot(_rms(h).astype(BF16), wpg_ref[...]))
            h = h + ple * gate
            o_ref[rows, :] = _rms(h) * gfin_ref[...]

        @pl.when(i == last)
        def _():
            for other in range(len(bufs)):
                if other != r:
                    wait_tile(bufs[other], sems.at[other])

    phase = lax.rem(i, len(bufs))
    for r in range(len(bufs)):
        pl.when(phase == r)(functools.partial(step, r))


def _combine_ple(dest_flat, h1, wts, p2, y, w_ple_in, g_ple, w_ple_gate, g_final, tm):
    n = h1.shape[0]
    const = lambda shape: pl.BlockSpec(shape, lambda i, d: (0,) * len(shape))
    return pl.pallas_call(
        functools.partial(_combine_kernel, tm=tm),
        grid_spec=pltpu.PrefetchScalarGridSpec(
            num_scalar_prefetch=1,
            grid=(n // tm,),
            in_specs=[
                pl.BlockSpec((tm, D_MODEL), lambda i, d: (i, 0)),
                pl.BlockSpec((tm, LANES), lambda i, d: (i, 0)),
                pl.BlockSpec((tm, PLE_DIM), lambda i, d: (i, 0)),
                pl.BlockSpec(memory_space=pl.ANY),
                const((PLE_DIM, D_MODEL)),
                const((1, D_MODEL)),
                const((D_MODEL, D_MODEL)),
                const((1, D_MODEL)),
            ],
            out_specs=pl.BlockSpec((tm, D_MODEL), lambda i, d: (i, 0)),
            scratch_shapes=[pltpu.VMEM((2, tm * ROW_TILE, LANES), U32)] * (COMBINE_LEAD + 1)
            + [pltpu.SemaphoreType.DMA((COMBINE_LEAD + 1,))],
        ),
        out_shape=jax.ShapeDtypeStruct((n, D_MODEL), F32),
        compiler_params=_cparams("arbitrary"),
        name="combine_ple",
    )(dest_flat, h1, wts, p2, y, w_ple_in, g_ple, w_ple_gate, g_final)


def _tile(n, pref):
    t = min(n, pref)
    assert n % t == 0, (n, t)
    return t


def kernel(x, p, g_mix, w_in, b_qkv, sinks, rel_bias, lb_logits, g_hgrn, w_up_att, w_up_rec, w_out, g_ffn,
           w_coarse, b_coarse, w_fine, b_fine, w_gate, w_up, w_down, w_ple_in, g_ple, w_ple_gate, g_final):
    B, T, D = x.shape
    assert D == D_MODEL and T % ATT_BLOCK == 0 and lb_logits.shape[0] == 2
    n = B * T
    layer = 0
    x2 = x.reshape(n, D)
    row = lambda a: a.reshape(1, -1)

    w_in_b = w_in[layer].astype(BF16)
    qkv = _qkv_proj(x2, row(g_mix[layer]), w_in_b[:, :ATT_QKV_W], row(b_qkv[layer]), _tile(n, 512))
    hz, f_raw = _hz_proj(x2, row(g_mix[layer]), w_in_b[:, ATT_QKV_W:], _tile(n, 1024))

    o_att = _swa(qkv, _bias_table(rel_bias), row(sinks[layer]), B, T)
    o_rec = _hgrn(hz, f_raw, lb_logits, row(g_hgrn[layer]), B, T, _tile(T, 1024))

    w_router = jnp.concatenate(
        [w_coarse[layer], w_fine[layer], jnp.zeros((D, ROUTER_W - MOE_GROUPS - MOE_EXPERTS), F32)], axis=1)
    b_router = jnp.concatenate(
        [b_coarse[layer], b_fine[layer], jnp.zeros((ROUTER_W - MOE_GROUPS - MOE_EXPERTS,), F32)]).reshape(1, -1)
    h1, xn_packed, logits = _mix_out(
        o_att, o_rec, hz, x2, w_up_att[layer].astype(BF16), w_up_rec[layer].astype(BF16),
        w_out[layer].astype(BF16), row(g_ffn[layer]), w_router, b_router, _tile(n, 256))

    cap = 2 * n + MOE_EXPERTS * EXPERT_ROWS
    nblk = cap // EXPERT_ROWS
    nblk_pad = (nblk + LANES - 1) // LANES * LANES
    dest, wts, blk_tab = _route(logits, _tile(n, 512), nblk_pad)
    dest_flat = dest.reshape(2 * n)
    blk_e = blk_tab[0, :nblk]
    n_used = blk_tab[1, 0:1]

    xs = _dispatch(dest_flat, blk_e, n_used, xn_packed, cap, _tile(n, 512))
    y = _experts(blk_e, n_used, xs, w_gate[layer], w_up[layer], w_down[layer])
    out = _combine_ple(dest_flat, h1, wts, p[layer].reshape(n, PLE_DIM), y, w_ple_in[layer].astype(BF16),
                       row(g_ple[layer]), w_ple_gate[layer].astype(BF16), row(g_final), _tile(n, 256))
    return out.reshape(B, T, D)
```

```python
import functools
import math

import numpy as np
import jax
import jax.numpy as jnp
from jax import lax
from jax.experimental import pallas as pl
from jax.experimental.pallas import tpu as pltpu

F32 = jnp.float32
BF16 = jnp.bfloat16
I32 = jnp.int32
U32 = jnp.uint32

D_MODEL = 2048
ATT_Q_HEADS = 16
ATT_KV_HEADS = 2
ATT_GROUP = ATT_Q_HEADS // ATT_KV_HEADS
ATT_HEAD_DIM = 64
WINDOW = 128
ATT_BLOCK = 128
ATT_Q_W = ATT_Q_HEADS * ATT_HEAD_DIM
ATT_KV_W = ATT_KV_HEADS * ATT_HEAD_DIM
ATT_QKV_W = ATT_Q_W + 2 * ATT_KV_W
REL_BUCKETS = 32
REL_MAX_DIST = 128
HGRN_HEADS = 8
HGRN_DK = 128
HGRN_DV = 128
HGRN_W = HGRN_HEADS * HGRN_DV
HGRN_CHUNK = 64
MOE_GROUPS = 8
MOE_EXPERTS_PER_GROUP = 8
MOE_EXPERTS = MOE_GROUPS * MOE_EXPERTS_PER_GROUP
MOE_HIDDEN = 512
PLE_DIM = 256
EPS = 1e-6
LOG2_E = math.log2(math.e)

HALF = D_MODEL // 2
ROUTER_W = 128
EXPERT_ROWS = 256
VMEM_LIMIT = 56 * 1024 * 1024


def _cparams(*sem):
    return pltpu.CompilerParams(dimension_semantics=sem, vmem_limit_bytes=VMEM_LIMIT)


def _rms(xf):
    return xf * lax.rsqrt(jnp.mean(xf * xf, axis=-1, keepdims=True) + EPS)


def _sigmoid(x):
    return 0.5 + 0.5 * jnp.tanh(0.5 * x)


def _silu(x):
    h = 0.5 * x
    return h + h * jnp.tanh(h)


def _pack_bf16_pair(lo_f32, hi_f32):
    lo = lax.bitcast_convert_type(lo_f32.astype(BF16).astype(F32), U32)
    hi = lax.bitcast_convert_type(hi_f32.astype(BF16).astype(F32), U32)
    return hi | (lo >> 16)


def _unpack_bf16_pair(w):
    lo = lax.bitcast_convert_type(w << 16, F32)
    hi = lax.bitcast_convert_type(w & jnp.uint32(0xFFFF0000), F32)
    return lo, hi


LANES = 128
ROW_TILE = HALF // LANES


def _store_token_rows(ref, words, first=0):
    tokens = words.shape[0]
    for c in range(ROW_TILE):
        ref[pl.ds(first * ROW_TILE + c, tokens, stride=ROW_TILE), :] = words[:, c * LANES:(c + 1) * LANES]


def _load_token_rows(ref, tokens, first=0):
    return jnp.concatenate([ref[pl.ds(first * ROW_TILE + c, tokens, stride=ROW_TILE), :]
                            for c in range(ROW_TILE)], axis=-1)


def _token_row(ref, idx):
    return ref.at[pl.ds(pl.multiple_of(idx * ROW_TILE, ROW_TILE), ROW_TILE), :]


def _qkv_kernel(x_ref, g_ref, w_ref, b_ref, o_ref):
    u = (_rms(x_ref[...]) * g_ref[...]).astype(BF16)
    acc = jnp.dot(u, w_ref[...], preferred_element_type=F32)
    o_ref[...] = (acc + b_ref[...]).astype(o_ref.dtype)


def _qkv_proj(x2, g_mix, w_qkv, b_qkv, tm):
    n = x2.shape[0]
    return pl.pallas_call(
        _qkv_kernel,
        grid=(n // tm,),
        in_specs=[
            pl.BlockSpec((tm, D_MODEL), lambda i: (i, 0)),
            pl.BlockSpec((1, D_MODEL), lambda i: (0, 0)),
            pl.BlockSpec((D_MODEL, ATT_QKV_W), lambda i: (0, 0)),
            pl.BlockSpec((1, ATT_QKV_W), lambda i: (0, 0)),
        ],
        out_specs=pl.BlockSpec((tm, ATT_QKV_W), lambda i: (i, 0)),
        out_shape=jax.ShapeDtypeStruct((n, ATT_QKV_W), BF16),
        compiler_params=_cparams("parallel"),
        name="qkv_proj",
    )(x2, g_mix, w_qkv, b_qkv)


HZ_TN = 1024
HZ_COLS = 4 * HGRN_W + 2 * D_MODEL
HZ_F_BLOCK = 1


def _hz_kernel(x_ref, g_ref, w_ref, hz_ref, f_ref, u_scr):
    j = pl.program_id(1)

    @pl.when(j == 0)
    def _():
        u_scr[...] = (_rms(x_ref[...]) * g_ref[...]).astype(BF16)

    acc = jnp.dot(u_scr[...], w_ref[...], preferred_element_type=F32)
    hz_ref[...] = acc.astype(BF16)

    @pl.when(j == HZ_F_BLOCK)
    def _():
        f_ref[...] = acc


def _hz_proj(x2, g_mix, w_hz, tm):
    n = x2.shape[0]
    return pl.pallas_call(
        _hz_kernel,
        grid=(n // tm, HZ_COLS // HZ_TN),
        in_specs=[
            pl.BlockSpec((tm, D_MODEL), lambda i, j: (i, 0)),
            pl.BlockSpec((1, D_MODEL), lambda i, j: (0, 0)),
            pl.BlockSpec((D_MODEL, HZ_TN), lambda i, j: (0, j)),
        ],
        out_specs=[
            pl.BlockSpec((tm, HZ_TN), lambda i, j: (i, j)),
            pl.BlockSpec((tm, HGRN_W), lambda i, j: (i, 0)),
        ],
        out_shape=[
            jax.ShapeDtypeStruct((n, HZ_COLS), BF16),
            jax.ShapeDtypeStruct((n, HGRN_W), F32),
        ],
        scratch_shapes=[pltpu.VMEM((tm, D_MODEL), BF16)],
        compiler_params=_cparams("parallel", "arbitrary"),
        name="hz_proj",
    )(x2, g_mix, w_hz)


def _t5_bucket_table():
    qi = np.arange(ATT_BLOCK)[:, None]
    kj = np.arange(2 * ATT_BLOCK)[None, :]
    dist = qi + ATT_BLOCK - kj
    exact = REL_BUCKETS // 2
    d = np.maximum(dist, 0)
    large = exact + (np.log(np.maximum(d, 1).astype(np.float32) / exact)
                     / math.log(REL_MAX_DIST / exact) * (REL_BUCKETS - exact)).astype(np.int32)
    large = np.minimum(large, REL_BUCKETS - 1)
    return np.where(d < exact, d, large).astype(np.int32)


def _bias_kernel(bucket_ref, rel_ref, o_ref):
    bk = bucket_ref[...]
    qi = lax.broadcasted_iota(I32, bk.shape, 0)
    kj = lax.broadcasted_iota(I32, bk.shape, 1)
    dist = qi + ATT_BLOCK - kj
    in_window = (dist >= 0) & (dist < WINDOW)
    first_block = in_window & (kj >= ATT_BLOCK)
    neg = jnp.float32(-jnp.inf)
    for h in range(ATT_Q_HEADS):
        acc = jnp.zeros(bk.shape, F32)
        for b in range(REL_BUCKETS):
            acc = jnp.where(bk == b, rel_ref[b, h], acc)
        o_ref[0, h] = jnp.where(first_block, acc, neg)
        o_ref[1, h] = jnp.where(in_window, acc, neg)


def _bias_table(rel_bias):
    bucket = jnp.asarray(_t5_bucket_table())
    return pl.pallas_call(
        _bias_kernel,
        in_specs=[
            pl.BlockSpec(memory_space=pltpu.VMEM),
            pl.BlockSpec(memory_space=pltpu.SMEM),
        ],
        out_specs=pl.BlockSpec(memory_space=pltpu.VMEM),
        out_shape=jax.ShapeDtypeStruct((2, ATT_Q_HEADS, ATT_BLOCK, 2 * ATT_BLOCK), F32),
        name="bias_table",
    )(bucket, rel_bias)


SWA_BLOCKS = 2


def _swa_kernel(q_ref, kvc_ref, kvp_ref, bias_ref, sink_ref, o_ref):
    hd = ATT_HEAD_DIM
    blk = ATT_BLOCK
    nt = (((1,), (1,)), ((), ()))
    step = pl.program_id(1)
    left = lax.broadcasted_iota(I32, (2 * blk, 2 * hd), 1) < hd
    left_q = lax.broadcasted_iota(I32, (blk, 2 * hd), 1) < hd
    one = jnp.ones((), F32)
    zero = jnp.zeros((), F32)
    for sb in range(SWA_BLOCKS):
        rows = slice(sb * blk, (sb + 1) * blk)
        prev = kvp_ref[...] if sb == 0 else kvc_ref[(sb - 1) * blk:sb * blk, :]
        band = jnp.concatenate([prev, kvc_ref[rows, :]], axis=0).astype(F32)
        kcol = band[:, :2 * hd] * (hd ** -0.5)
        vcol = band[:, 2 * hd:]
        krot = pltpu.roll(kcol, hd, axis=1)
        vrot = pltpu.roll(vcol, hd, axis=1)
        table = jnp.minimum(step * SWA_BLOCKS + sb, 1) if sb == 0 else 1
        for kk in range(ATT_KV_HEADS):
            k_src, k_alt = (kcol, krot) if kk == 0 else (krot, kcol)
            v_src, v_alt = (vcol, vrot) if kk == 0 else (vrot, vcol)
            k_a = jnp.where(left, k_src, zero).astype(BF16)
            k_b = jnp.where(left, zero, k_alt).astype(BF16)
            v_a = jnp.where(left, v_src, one).astype(BF16)
            v_b = jnp.where(left, one, v_alt).astype(BF16)
            pairs = range(kk * ATT_GROUP // 2, (kk + 1) * ATT_GROUP // 2)
            scores = []
            for j in pairs:
                qp = q_ref[rows, 2 * hd * j:2 * hd * (j + 1)]
                scores.append((lax.dot_general(qp, k_a, nt, preferred_element_type=F32),
                               lax.dot_general(qp, k_b, nt, preferred_element_type=F32)))
            probs, sink_terms = [], []
            for j, pair in zip(pairs, scores):
                p_pair, t_pair = [], []
                for h, s in zip((2 * j, 2 * j + 1), pair):
                    s = s + bias_ref[table, h]
                    sink = sink_ref[0, h]
                    m = jnp.maximum(jnp.max(s, axis=-1, keepdims=True), sink)
                    p_pair.append(jnp.exp(s - m).astype(BF16))
                    t_pair.append(jnp.exp(sink - m))
                probs.append(p_pair)
                sink_terms.append(t_pair)
            ext = [(jnp.dot(pa, v_a, preferred_element_type=F32), jnp.dot(pb, v_b, preferred_element_type=F32))
                   for pa, pb in probs]
            for j, (ea, eb), (ta, tb) in zip(pairs, ext, sink_terms):
                num = jnp.where(left_q, ea, eb)
                den = pltpu.roll(jnp.where(left_q, eb, ea), hd, axis=1)
                den = den + jnp.where(left_q, ta, tb)
                o_ref[rows, 2 * hd * j:2 * hd * (j + 1)] = (num / den).astype(o_ref.dtype)


def _swa(qkv, bias_tab, sinks, batch, seq):
    n = qkv.shape[0]
    nb = seq // ATT_BLOCK
    assert nb % SWA_BLOCKS == 0
    ns = nb // SWA_BLOCKS
    rows = SWA_BLOCKS * ATT_BLOCK
    kv_col = ATT_Q_W // (2 * ATT_KV_W)

    return pl.pallas_call(
        _swa_kernel,
        grid=(batch, ns),
        in_specs=[
            pl.BlockSpec((rows, ATT_Q_W), lambda b, i: (b * ns + i, 0)),
            pl.BlockSpec((rows, 2 * ATT_KV_W), lambda b, i: (b * ns + i, kv_col)),
            pl.BlockSpec((ATT_BLOCK, 2 * ATT_KV_W),
                         lambda b, i: (b * nb + jnp.maximum(i * SWA_BLOCKS - 1, 0), kv_col)),
            pl.BlockSpec((2, ATT_Q_HEADS, ATT_BLOCK, 2 * ATT_BLOCK), lambda b, i: (0, 0, 0, 0)),
            pl.BlockSpec(memory_space=pltpu.SMEM),
        ],
        out_specs=pl.BlockSpec((rows, ATT_Q_W), lambda b, i: (b * ns + i, 0)),
        out_shape=jax.ShapeDtypeStruct((n, ATT_Q_W), BF16),
        compiler_params=_cparams("parallel", "arbitrary"),
        name="swa",
    )(qkv, qkv, qkv, bias_tab, sinks)


def _cumsum_rows(tri_bf16, g):
    w = g.shape[1]
    g1 = g.astype(BF16)
    r1 = g - g1.astype(F32)
    g2 = r1.astype(BF16)
    g3 = (r1 - g2.astype(F32)).astype(BF16)
    parts = jnp.dot(tri_bf16, jnp.concatenate([g1, g2, g3], axis=1), preferred_element_type=F32)
    return parts[:, :w] + (parts[:, w:2 * w] + parts[:, 2 * w:])


def _hgrn_kernel(q_ref, f_ref, i_ref, g_ref, lbl_ref, gh_ref, o_ref, st_ref, b_scr, k_scr, *, chunks):
    C = HGRN_CHUNK
    ref_row = C // 2 - 1

    @pl.when(pl.program_id(1) == 0)
    def _():
        st_ref[...] = jnp.zeros_like(st_ref)

    l = lbl_ref[...]
    e = jnp.exp(l - jnp.max(l, axis=0, keepdims=True))
    lb_all = e[0:1, :] / jnp.sum(e, axis=0, keepdims=True)
    f_mid = 0.5 * (1.0 + lb_all)
    f_half = 0.5 * (1.0 - lb_all)

    row = lax.broadcasted_iota(I32, (C, C), 0)
    col = lax.broadcasted_iota(I32, (C, C), 1)
    causal = row >= col
    tri = causal.astype(F32).astype(BF16)
    gain = gh_ref[...]

    def chunk_rows(c):
        return pl.ds(pl.multiple_of(c * C, C), C)

    def decays(c, slot):
        swing = f_half * jnp.tanh(0.5 * f_ref[chunk_rows(c), :])
        k_scr[slot] = f_half - swing
        b_scr[slot] = _cumsum_rows(tri, jnp.log(f_mid + swing)) * LOG2_E

    def outputs(c, slot):
        rows = chunk_rows(c)
        b = b_scr[slot]
        k = k_scr[slot]
        qs = _silu(q_ref[rows, :].astype(F32))
        b_ref = b[ref_row:ref_row + 1, :]
        b_last = b[C - 1:C, :]
        qe_all = (qs * jnp.exp2(b)).astype(BF16)
        qa_all = (qs * jnp.exp2(b - b_ref)).astype(BF16)
        ka_all = (k * jnp.exp2(b_ref - b)).astype(BF16)
        kl_all = (k * jnp.exp2(b_last - b)).astype(BF16)
        decay = jnp.exp2(b_last)
        gate = _silu(g_ref[rows, :].astype(F32))
        nt = (((1,), (1,)), ((), ()))
        heads = [slice(HGRN_DK * h, HGRN_DK * (h + 1)) for h in range(HGRN_HEADS)]
        vs = [i_ref[rows, cs] for cs in heads]
        sts = [st_ref[h] for h in range(HGRN_HEADS)]
        attn = [lax.dot_general(qa_all[:, cs], ka_all[:, cs], nt, preferred_element_type=F32) for cs in heads]
        inter = [lax.dot_general(qe_all[:, cs], st.astype(BF16), nt, preferred_element_type=F32)
                 for cs, st in zip(heads, sts)]
        upd = [lax.dot_general(v, kl_all[:, cs], (((0,), (0,)), ((), ())), preferred_element_type=F32)
               for cs, v in zip(heads, vs)]
        for h, cs in enumerate(heads):
            st_ref[h] = sts[h] * decay[:, cs] + upd[h]
        intra = [jnp.dot(jnp.where(causal, a, 0.0).astype(BF16), v, preferred_element_type=F32)
                 for a, v in zip(attn, vs)]
        for h, cs in enumerate(heads):
            o = inter[h] + intra[h]
            o_ref[rows, cs] = (_rms(o) * gain * gate[:, cs]).astype(o_ref.dtype)

    decays(0, 0)

    def pair(p, carry):
        c = 2 * p
        decays(c + 1, 1)
        outputs(c, 0)
        decays(jnp.minimum(c + 2, chunks - 1), 0)
        outputs(c + 1, 1)
        return carry

    lax.fori_loop(0, chunks // 2, pair, 0)


def _hgrn(hz, f_raw, lb_logits, g_hgrn, batch, seq, tc):
    n = hz.shape[0]
    nt = seq // tc
    blk = lambda col: pl.BlockSpec((tc, HGRN_W), lambda b, t, col=col: (b * nt + t, col))
    return pl.pallas_call(
        functools.partial(_hgrn_kernel, chunks=tc // HGRN_CHUNK),
        grid=(batch, nt),
        in_specs=[
            blk(0),
            pl.BlockSpec((tc, HGRN_W), lambda b, t: (b * nt + t, 0)),
            blk(2),
            blk(3),
            pl.BlockSpec((2, HGRN_W), lambda b, t: (0, 0)),
            pl.BlockSpec((1, HGRN_DV), lambda b, t: (0, 0)),
        ],
        out_specs=pl.BlockSpec((tc, HGRN_W), lambda b, t: (b * nt + t, 0)),
        out_shape=jax.ShapeDtypeStruct((n, HGRN_W), BF16),
        scratch_shapes=[pltpu.VMEM((HGRN_HEADS, HGRN_DV, HGRN_DK), F32),
                        pltpu.VMEM((2, HGRN_CHUNK, HGRN_W), F32), pltpu.VMEM((2, HGRN_CHUNK, HGRN_W), F32)],
        compiler_params=_cparams("parallel", "arbitrary"),
        name="hgrn",
    )(hz, f_raw, hz, hz, lb_logits, g_hgrn)


def _split_hi_lo(a):
    hi = a.astype(BF16)
    lo = (a - hi.astype(F32)).astype(BF16)
    return hi, lo


def _mix_kernel(oa_ref, or_ref, ga_ref, gr_ref, x_ref, wa_ref, wr_ref, wo_ref, gf_ref, wrt_ref, brt_ref,
                h_ref, xn_ref, lg_ref):
    dot = functools.partial(jnp.dot, preferred_element_type=F32)
    w_hi, w_lo = _split_hi_lo(wrt_ref[...])
    tm = x_ref.shape[0]
    sub = tm // MIX_SUBTILES
    for r in range(MIX_SUBTILES):
        rows = slice(r * sub, (r + 1) * sub)
        ya = dot(oa_ref[rows, :], wa_ref[...])
        yr = dot(or_ref[rows, :], wr_ref[...])
        mix = _sigmoid(ga_ref[rows, :].astype(F32)) * ya + _sigmoid(gr_ref[rows, :].astype(F32)) * yr
        h = x_ref[rows, :] + dot(mix.astype(BF16), wo_ref[...])
        h_ref[rows, :] = h
        xn = _rms(h) * gf_ref[...]
        _store_token_rows(xn_ref, _pack_bf16_pair(xn[:, :HALF], xn[:, HALF:]), first=r * sub)
        x_hi, x_lo = _split_hi_lo(xn)
        lg_ref[rows, :] = dot(x_hi, w_hi) + (dot(x_hi, w_lo) + dot(x_lo, w_hi)) + brt_ref[...]


def _mix_out(o_att, o_rec, hz, x2, w_up_att, w_up_rec, w_out, g_ffn, w_router, b_router, tm):
    n = x2.shape[0]
    const = lambda shape: pl.BlockSpec(shape, lambda i: (0,) * len(shape))
    ga_col = 4 * HGRN_W // D_MODEL
    return pl.pallas_call(
        _mix_kernel,
        grid=(n // tm,),
        in_specs=[
            pl.BlockSpec((tm, ATT_Q_W), lambda i: (i, 0)),
            pl.BlockSpec((tm, HGRN_W), lambda i: (i, 0)),
            pl.BlockSpec((tm, D_MODEL), lambda i: (i, ga_col)),
            pl.BlockSpec((tm, D_MODEL), lambda i: (i, ga_col + 1)),
            pl.BlockSpec((tm, D_MODEL), lambda i: (i, 0)),
            const((ATT_Q_W, D_MODEL)),
            const((HGRN_W, D_MODEL)),
            const((D_MODEL, D_MODEL)),
            const((1, D_MODEL)),
            const((D_MODEL, ROUTER_W)),
            const((1, ROUTER_W)),
        ],
        out_specs=[
            pl.BlockSpec((tm, D_MODEL), lambda i: (i, 0)),
            pl.BlockSpec((tm * ROW_TILE, LANES), lambda i: (i, 0)),
            pl.BlockSpec((tm, ROUTER_W), lambda i: (i, 0)),
        ],
        out_shape=[
            jax.ShapeDtypeStruct((n, D_MODEL), F32),
            jax.ShapeDtypeStruct((n * ROW_TILE, LANES), U32),
            jax.ShapeDtypeStruct((n, ROUTER_W), F32),
        ],
        compiler_params=_cparams("parallel"),
        name="mix_out",
    )(o_att, o_rec, hz, hz, x2, w_up_att, w_up_rec, w_out, g_ffn, w_router, b_router)


def _route_choice(lgt):
    row = lax.broadcasted_iota(I32, lgt.shape, 0).astype(F32)
    neg = jnp.float32(-jnp.inf)
    far = float(ROUTER_W)
    lc = jnp.where(row < MOE_GROUPS, lgt, neg)
    cmax = jnp.max(lc, axis=0, keepdims=True)
    p_sel = 1.0 / jnp.sum(jnp.exp(lc - cmax), axis=0, keepdims=True)
    grp = jnp.min(jnp.where(lc == cmax, row, far), axis=0, keepdims=True)
    fine = row - MOE_GROUPS
    first = grp * MOE_EXPERTS_PER_GROUP
    lf = jnp.where((fine >= first) & (fine < first + MOE_EXPERTS_PER_GROUP), lgt, neg)
    v0 = jnp.max(lf, axis=0, keepdims=True)
    sel0 = fine == jnp.min(jnp.where(lf == v0, fine, far), axis=0, keepdims=True)
    lf1 = jnp.where(sel0, neg, lf)
    v1 = jnp.max(lf1, axis=0, keepdims=True)
    sel1 = fine == jnp.min(jnp.where(lf1 == v1, fine, far), axis=0, keepdims=True)
    t1 = jnp.exp(v1 - v0)
    w0 = p_sel / (1.0 + t1)
    w1 = p_sel * t1 / (1.0 + t1)
    return sel0, sel1, w0, w1


def _route_kernel(lg_ref, dest_ref, wts_ref, blk_ref, cnt_ref, *, tm, nblk_pad):
    phase = pl.program_id(0)
    i = pl.program_id(1)
    R = ROUTER_W

    @pl.when((phase == 0) & (i == 0))
    def _():
        cnt_ref[...] = jnp.zeros_like(cnt_ref)

    sel0, sel1, w0, w1 = _route_choice(lg_ref[...].T)
    picks = (sel0 | sel1).astype(F32)
    picked = jnp.sum(picks, axis=1, keepdims=True)

    @pl.when(phase == 0)
    def _():
        cnt_ref[0] = cnt_ref[0] + picked

    @pl.when(phase == 1)
    def _():
        dot = functools.partial(jnp.dot, preferred_element_type=F32)
        nb = jnp.floor((cnt_ref[0] + (EXPERT_ROWS - 1)) * (1.0 / EXPERT_ROWS))
        nb_hi = jnp.floor(nb * (1.0 / 32.0))
        nb_lo = nb - 32.0 * nb_hi
        r = lax.broadcasted_iota(I32, (R, R), 0)
        c = lax.broadcasted_iota(I32, (R, R), 1)
        lower = (c < r).astype(F32).astype(BF16)
        start_blk = 32.0 * dot(lower, nb_hi.astype(BF16)) + dot(lower, nb_lo.astype(BF16))

        @pl.when(i == 0)
        def _():
            cnt_ref[1] = jnp.zeros((R, LANES), F32)
            end_col = (start_blk + nb)[:, 0:1]
            erow = lax.broadcasted_iota(I32, (R, nblk_pad), 0)
            is_expert = (erow >= MOE_GROUPS) & (erow < MOE_GROUPS + MOE_EXPERTS)
            bidx = lax.broadcasted_iota(I32, (R, nblk_pad), 1).astype(F32)
            be = jnp.sum((is_expert & (end_col <= bidx)).astype(F32), axis=0, keepdims=True)
            be = jnp.minimum(be, float(MOE_EXPERTS - 1))
            n_used = jnp.sum(nb[:, 0:1], axis=0, keepdims=True)
            out_row = lax.broadcasted_iota(I32, blk_ref.shape, 0)
            blk_ref[...] = jnp.where(out_row == 0, be, n_used).astype(I32)

        tr = lax.broadcasted_iota(I32, (tm, tm), 0)
        tc = lax.broadcasted_iota(I32, (tm, tm), 1)
        earlier = (tr < tc).astype(F32).astype(BF16)
        before = dot(picks.astype(BF16), earlier)
        pos = (start_blk * float(EXPERT_ROWS) + cnt_ref[1])[:, 0:1] + before
        dest_ref[0:1, :] = jnp.sum(jnp.where(sel0, pos, 0.0), axis=0, keepdims=True).astype(I32)
        dest_ref[1:2, :] = jnp.sum(jnp.where(sel1, pos, 0.0), axis=0, keepdims=True).astype(I32)
        cnt_ref[1] = cnt_ref[1] + picked
        wrow = lax.broadcasted_iota(I32, (R, tm), 0)
        wts_ref[...] = jnp.where(wrow == 0, w0, jnp.where(wrow == 1, w1, 0.0)).T


def _route(logits, tm, nblk_pad):
    n = logits.shape[0]
    return pl.pallas_call(
        functools.partial(_route_kernel, tm=tm, nblk_pad=nblk_pad),
        grid=(2, n // tm),
        in_specs=[pl.BlockSpec((tm, ROUTER_W), lambda p, i: (i, 0))],
        out_specs=[
            pl.BlockSpec((2, tm), lambda p, i: (0, i * p)),
            pl.BlockSpec((tm, LANES), lambda p, i: (i * p, 0)),
            pl.BlockSpec((8, nblk_pad), lambda p, i: (0, 0)),
        ],
        out_shape=[
            jax.ShapeDtypeStruct((2, n), I32),
            jax.ShapeDtypeStruct((n, LANES), F32),
            jax.ShapeDtypeStruct((8, nblk_pad), I32),
        ],
        scratch_shapes=[pltpu.VMEM((2, ROUTER_W, LANES), F32)],
        compiler_params=_cparams("arbitrary", "arbitrary"),
        name="route",
    )(logits)


ISSUE_UNROLL = 8
COMBINE_SUBTILES = 2
COMBINE_LEAD = 2
MIX_SUBTILES = 1


def _dispatch_kernel(dest_ref, blk_ref, nused_ref, xn_ref, xs_ref, zbuf, sem, zsem, *, tm, nblk):
    i = pl.program_id(0)
    base = i * tm
    block_rows = EXPERT_ROWS * ROW_TILE

    @pl.when(i == 0)
    def _():
        zbuf[...] = jnp.zeros_like(zbuf)
        last_used = nused_ref[0] - 1

        def needs_clear(j):
            return (j >= last_used) | (blk_ref[j] != blk_ref[jnp.minimum(j + 1, nblk - 1)])

        def clear(j):
            rows = pl.ds(pl.multiple_of(j * block_rows, block_rows), block_rows)
            return pltpu.make_async_copy(zbuf, xs_ref.at[rows, :], zsem)

        def start(j, c):
            @pl.when(needs_clear(j))
            def _():
                clear(j).start()
            return c

        def wait(j, c):
            @pl.when(needs_clear(j))
            def _():
                clear(j).wait()
            return c

        lax.fori_loop(0, nblk, start, 0)
        lax.fori_loop(0, nblk, wait, 0)

    n_tokens = tm * pl.num_programs(0)

    def issue(t, c):
        for k in range(2):
            pltpu.make_async_copy(_token_row(xn_ref, t),
                                  _token_row(xs_ref, dest_ref[k * n_tokens + base + t]), sem).start(priority=k)
        return c

    lax.fori_loop(0, tm, issue, 0, unroll=ISSUE_UNROLL)
    for k in range(2):
        pltpu.make_async_copy(xn_ref, xs_ref.at[pl.ds(0, tm * ROW_TILE), :], sem).wait()


def _dispatch(dest_flat, blk_e, n_used, xn_packed, cap, tm):
    n = xn_packed.shape[0] // ROW_TILE
    nblk = cap // EXPERT_ROWS
    return pl.pallas_call(
        functools.partial(_dispatch_kernel, tm=tm, nblk=nblk),
        grid_spec=pltpu.PrefetchScalarGridSpec(
            num_scalar_prefetch=3,
            grid=(n // tm,),
            in_specs=[pl.BlockSpec((tm * ROW_TILE, LANES), lambda i, d, b, u: (i, 0))],
            out_specs=pl.BlockSpec(memory_space=pl.ANY),
            scratch_shapes=[pltpu.VMEM((EXPERT_ROWS * ROW_TILE, LANES), U32),
                            pltpu.SemaphoreType.DMA, pltpu.SemaphoreType.DMA],
        ),
        out_shape=jax.ShapeDtypeStruct((cap * ROW_TILE, LANES), U32),
        compiler_params=_cparams("arbitrary"),
        name="dispatch",
    )(dest_flat, blk_e, n_used, xn_packed)


def _expert_kernel(blk_ref, nused_ref, xs_ref, wg_hbm, wu_hbm, wd_hbm, y_ref,
                   wg_f, wu_f, wd_f, wg_b, wu_b, wd_b, slot_ref, sems, *, nblk):
    i = pl.program_id(0)
    nused = nused_ref[0]
    active = i < nused
    expert = blk_ref[i]
    new_expert = (i == 0) | (expert != blk_ref[jnp.maximum(i - 1, 0)])

    def fetch(e, slot):
        return [pltpu.make_async_copy(w_hbm.at[e], w_f.at[slot], sems.at[slot])
                for w_hbm, w_f in ((wg_hbm, wg_f), (wu_hbm, wu_f), (wd_hbm, wd_f))]

    @pl.when(jnp.logical_not(active))
    def _():
        y_ref[...] = jnp.zeros_like(y_ref)

    @pl.when(i == 0)
    def _():
        slot_ref[0] = 0
        for c in fetch(expert, 0):
            c.start()

    @pl.when(active & new_expert)
    def _():
        slot = slot_ref[0]
        for c in fetch(expert, slot):
            c.wait()
        wg_b[...] = wg_f[slot].astype(BF16)
        wu_b[...] = wu_f[slot].astype(BF16)
        wd_b[...] = wd_f[slot].astype(BF16)
        nxt = lax.while_loop(lambda j: (j < nused) & (blk_ref[jnp.minimum(j, nblk - 1)] == expert),
                             lambda j: j + 1, i + 1)

        @pl.when(nxt < nused)
        def _():
            for c in fetch(blk_ref[jnp.minimum(nxt, nblk - 1)], 1 - slot):
                c.start()

        slot_ref[0] = 1 - slot

    @pl.when(active)
    def _():
        dot = functools.partial(jnp.dot, preferred_element_type=F32)
        lo, hi = _unpack_bf16_pair(_load_token_rows(xs_ref, EXPERT_ROWS))
        lo = lo.astype(BF16)
        hi = hi.astype(BF16)
        g = dot(lo, wg_b[:HALF, :]) + dot(hi, wg_b[HALF:, :])
        u = dot(lo, wu_b[:HALF, :]) + dot(hi, wu_b[HALF:, :])
        hid = (_silu(g) * u).astype(BF16)
        y = dot(hid, wd_b[...])
        _store_token_rows(y_ref, _pack_bf16_pair(y[:, :HALF], y[:, HALF:]))


def _experts(blk_e, n_used, xs, w_gate, w_up, w_down):
    cap = xs.shape[0] // ROW_TILE
    nblk = cap // EXPERT_ROWS
    row = lambda i, blk, nu: (jnp.minimum(i, nu[0] - 1), 0)
    hbm = pl.BlockSpec(memory_space=pl.ANY)
    up_shape, down_shape = (D_MODEL, MOE_HIDDEN), (MOE_HIDDEN, D_MODEL)
    return pl.pallas_call(
        functools.partial(_expert_kernel, nblk=nblk),
        grid_spec=pltpu.PrefetchScalarGridSpec(
            num_scalar_prefetch=2,
            grid=(nblk,),
            in_specs=[pl.BlockSpec((EXPERT_ROWS * ROW_TILE, LANES), row), hbm, hbm, hbm],
            out_specs=pl.BlockSpec((EXPERT_ROWS * ROW_TILE, LANES), lambda i, blk, nu: (i, 0)),
            scratch_shapes=[
                pltpu.VMEM((2,) + up_shape, F32), pltpu.VMEM((2,) + up_shape, F32),
                pltpu.VMEM((2,) + down_shape, F32),
                pltpu.VMEM(up_shape, BF16), pltpu.VMEM(up_shape, BF16), pltpu.VMEM(down_shape, BF16),
                pltpu.SMEM((1,), I32), pltpu.SemaphoreType.DMA((2,)),
            ],
        ),
        out_shape=jax.ShapeDtypeStruct((cap * ROW_TILE, LANES), U32),
        compiler_params=_cparams("arbitrary"),
        name="experts",
    )(blk_e, n_used, xs, w_gate, w_up, w_down)


def _combine_kernel(dest_ref, h_ref, wts_ref, p_ref, y_ref, wpi_ref, gp_ref, wpg_ref, gfin_ref, o_ref,
                    *scratch, tm):
    *bufs, sems = scratch
    i = pl.program_id(0)
    last = pl.num_programs(0) - 1
    n_tokens = tm * pl.num_programs(0)
    dot = functools.partial(jnp.dot, preferred_element_type=F32)

    def row_copy(tile, t, k, buf, sem):
        return pltpu.make_async_copy(_token_row(y_ref, dest_ref[k * n_tokens + tile * tm + t]),
                                     _token_row(buf.at[k], t), sem)

    def wait_tile(buf, sem):
        for k in range(2):
            pltpu.make_async_copy(y_ref.at[pl.ds(0, tm * ROW_TILE), :], buf.at[k], sem).wait()

    @pl.when(i == 0)
    def _():
        def issue(t, c):
            for tile in range(COMBINE_LEAD):
                for k in range(2):
                    row_copy(jnp.minimum(tile, last), t, k, bufs[tile], sems.at[tile]).start(priority=1)
            return c

        lax.fori_loop(0, tm, issue, 0, unroll=ISSUE_UNROLL)

    def step(r):
        cur, cur_sem = bufs[r], sems.at[r]
        ahead = (r + COMBINE_LEAD) % len(bufs)
        nxt, nxt_sem = bufs[ahead], sems.at[ahead]
        wait_tile(cur, cur_sem)
        nxt_tile = jnp.minimum(i + COMBINE_LEAD, last)
        for t in range(tm):
            for k in range(2):
                row_copy(nxt_tile, t, k, nxt, nxt_sem).start(priority=1)

        sub = tm // COMBINE_SUBTILES
        for r in range(COMBINE_SUBTILES):
            rows = slice(r * sub, (r + 1) * sub)
            ple = _rms(dot(p_ref[rows, :].astype(BF16), wpi_ref[...])) * gp_ref[...]
            w = wts_ref[rows, :]
            w0 = w[:, 0:1]
            w1 = w[:, 1:2]
            lo0, hi0 = _unpack_bf16_pair(_load_token_rows(cur.at[0], sub, first=r * sub))
            lo1, hi1 = _unpack_bf16_pair(_load_token_rows(cur.at[1], sub, first=r * sub))
            moe = jnp.concatenate([w0 * lo0 + w1 * lo1, w0 * hi0 + w1 * hi1], axis=-1)
            h = h_ref[rows, :] + moe
            gate = _sigmoid(dot(_rms(h).astype(BF16), wpg_ref[...]))
            h = h + ple * gate
            o_ref[rows, :] = _rms(h) * gfin_ref[...]

        @pl.when(i == last)
        def _():
            for other in range(len(bufs)):
                if other != r:
                    wait_tile(bufs[other], sems.at[other])

    phase = lax.rem(i, len(bufs))
    for r in range(len(bufs)):
        pl.when(phase == r)(functools.partial(step, r))


def _combine_ple(dest_flat, h1, wts, p2, y, w_ple_in, g_ple, w_ple_gate, g_final, tm):
    n = h1.shape[0]
    const = lambda shape: pl.BlockSpec(shape, lambda i, d: (0,) * len(shape))
    return pl.pallas_call(
        functools.partial(_combine_kernel, tm=tm),
        grid_spec=pltpu.PrefetchScalarGridSpec(
            num_scalar_prefetch=1,
            grid=(n // tm,),
            in_specs=[
                pl.BlockSpec((tm, D_MODEL), lambda i, d: (i, 0)),
                pl.BlockSpec((tm, LANES), lambda i, d: (i, 0)),
                pl.BlockSpec((tm, PLE_DIM), lambda i, d: (i, 0)),
                pl.BlockSpec(memory_space=pl.ANY),
                const((PLE_DIM, D_MODEL)),
                const((1, D_MODEL)),
                const((D_MODEL, D_MODEL)),
                const((1, D_MODEL)),
            ],
            out_specs=pl.BlockSpec((tm, D_MODEL), lambda i, d: (i, 0)),
            scratch_shapes=[pltpu.VMEM((2, tm * ROW_TILE, LANES), U32)] * (COMBINE_LEAD + 1)
            + [pltpu.SemaphoreType.DMA((COMBINE_LEAD + 1,))],
        ),
        out_shape=jax.ShapeDtypeStruct((n, D_MODEL), F32),
        compiler_params=_cparams("arbitrary"),
        name="combine_ple",
    )(dest_flat, h1, wts, p2, y, w_ple_in, g_ple, w_ple_gate, g_final)


def _tile(n, pref):
    t = min(n, pref)
    assert n % t == 0, (n, t)
    return t


def kernel(x, p, g_mix, w_in, b_qkv, sinks, rel_bias, lb_logits, g_hgrn, w_up_att, w_up_rec, w_out, g_ffn,
           w_coarse, b_coarse, w_fine, b_fine, w_gate, w_up, w_down, w_ple_in, g_ple, w_ple_gate, g_final):
    B, T, D = x.shape
    assert D == D_MODEL and T % ATT_BLOCK == 0 and lb_logits.shape[0] == 2
    n = B * T
    layer = 0
    x2 = x.reshape(n, D)
    row = lambda a: a.reshape(1, -1)

    w_qkv = w_in[layer][:, :ATT_QKV_W].astype(BF16)
    w_hz = w_in[layer][:, ATT_QKV_W:].astype(BF16)
    qkv = _qkv_proj(x2, row(g_mix[layer]), w_qkv, row(b_qkv[layer]), _tile(n, 1024))
    hz, f_raw = _hz_proj(x2, row(g_mix[layer]), w_hz, _tile(n, 1024))

    o_att = _swa(qkv, _bias_table(rel_bias), row(sinks[layer]), B, T)
    o_rec = _hgrn(hz, f_raw, lb_logits, row(g_hgrn[layer]), B, T, _tile(T, 1024))

    w_router = jnp.concatenate(
        [w_coarse[layer], w_fine[layer], jnp.zeros((D, ROUTER_W - MOE_GROUPS - MOE_EXPERTS), F32)], axis=1)
    b_router = jnp.concatenate(
        [b_coarse[layer], b_fine[layer], jnp.zeros((ROUTER_W - MOE_GROUPS - MOE_EXPERTS,), F32)]).reshape(1, -1)
    h1, xn_packed, logits = _mix_out(
        o_att, o_rec, hz, x2, w_up_att[layer].astype(BF16), w_up_rec[layer].astype(BF16),
        w_out[layer].astype(BF16), row(g_ffn[layer]), w_router, b_router, _tile(n, 256))

    cap = 2 * n + MOE_EXPERTS * EXPERT_ROWS
    nblk = cap // EXPERT_ROWS
    nblk_pad = (nblk + LANES - 1) // LANES * LANES
    dest, wts, blk_tab = _route(logits, _tile(n, 512), nblk_pad)
    dest_flat = dest.reshape(2 * n)
    blk_e = blk_tab[0, :nblk]
    n_used = blk_tab[1, 0:1]

    xs = _dispatch(dest_flat, blk_e, n_used, xn_packed, cap, _tile(n, 512))
    y = _experts(blk_e, n_used, xs, w_gate[layer], w_up[layer], w_down[layer])
    out = _combine_ple(dest_flat, h1, wts, p[layer].reshape(n, PLE_DIM), y, w_ple_in[layer].astype(BF16),
                       row(g_ple[layer]), w_ple_gate[layer].astype(BF16), row(g_final), _tile(n, 256))
    return out.reshape(B, T, D)
```

```python
import functools
import math

import numpy as np
import jax
import jax.numpy as jnp
from jax import lax
from jax.experimental import pallas as pl
from jax.experimental.pallas import tpu as pltpu

F32 = jnp.float32
BF16 = jnp.bfloat16
I32 = jnp.int32
U32 = jnp.uint32

D_MODEL = 2048
ATT_Q_HEADS = 16
ATT_KV_HEADS = 2
ATT_GROUP = ATT_Q_HEADS // ATT_KV_HEADS
ATT_HEAD_DIM = 64
WINDOW = 128
ATT_BLOCK = 128
ATT_Q_W = ATT_Q_HEADS * ATT_HEAD_DIM
ATT_KV_W = ATT_KV_HEADS * ATT_HEAD_DIM
ATT_QKV_W = ATT_Q_W + 2 * ATT_KV_W
REL_BUCKETS = 32
REL_MAX_DIST = 128
HGRN_HEADS = 8
HGRN_DK = 128
HGRN_DV = 128
HGRN_W = HGRN_HEADS * HGRN_DV
HGRN_CHUNK = 64
MOE_GROUPS = 8
MOE_EXPERTS_PER_GROUP = 8
MOE_EXPERTS = MOE_GROUPS * MOE_EXPERTS_PER_GROUP
MOE_HIDDEN = 512
PLE_DIM = 256
EPS = 1e-6
LOG2_E = math.log2(math.e)

HALF = D_MODEL // 2
ROUTER_W = 128
EXPERT_ROWS = 256
VMEM_LIMIT = 56 * 1024 * 1024


def _cparams(*sem):
    return pltpu.CompilerParams(dimension_semantics=sem, vmem_limit_bytes=VMEM_LIMIT)


def _rms(xf):
    return xf * lax.rsqrt(jnp.mean(xf * xf, axis=-1, keepdims=True) + EPS)


def _sigmoid(x):
    return 0.5 + 0.5 * jnp.tanh(0.5 * x)


def _silu(x):
    h = 0.5 * x
    return h + h * jnp.tanh(h)


def _pack_bf16_pair(lo_f32, hi_f32):
    lo = lax.bitcast_convert_type(lo_f32.astype(BF16).astype(F32), U32)
    hi = lax.bitcast_convert_type(hi_f32.astype(BF16).astype(F32), U32)
    return hi | (lo >> 16)


def _unpack_bf16_pair(w):
    lo = lax.bitcast_convert_type(w << 16, F32)
    hi = lax.bitcast_convert_type(w & jnp.uint32(0xFFFF0000), F32)
    return lo, hi


LANES = 128
ROW_TILE = HALF // LANES


def _store_token_rows(ref, words, first=0):
    tokens = words.shape[0]
    for c in range(ROW_TILE):
        ref[pl.ds(first * ROW_TILE + c, tokens, stride=ROW_TILE), :] = words[:, c * LANES:(c + 1) * LANES]


def _load_token_rows(ref, tokens, first=0):
    return jnp.concatenate([ref[pl.ds(first * ROW_TILE + c, tokens, stride=ROW_TILE), :]
                            for c in range(ROW_TILE)], axis=-1)


def _token_row(ref, idx):
    return ref.at[pl.ds(pl.multiple_of(idx * ROW_TILE, ROW_TILE), ROW_TILE), :]


def _qkv_kernel(x_ref, g_ref, w_ref, b_ref, o_ref):
    u = (_rms(x_ref[...]) * g_ref[...]).astype(BF16)
    acc = jnp.dot(u, w_ref[...], preferred_element_type=F32)
    o_ref[...] = (acc + b_ref[...]).astype(o_ref.dtype)


def _qkv_proj(x2, g_mix, w_qkv, b_qkv, tm):
    n = x2.shape[0]
    return pl.pallas_call(
        _qkv_kernel,
        grid=(n // tm,),
        in_specs=[
            pl.BlockSpec((tm, D_MODEL), lambda i: (i, 0)),
            pl.BlockSpec((1, D_MODEL), lambda i: (0, 0)),
            pl.BlockSpec((D_MODEL, ATT_QKV_W), lambda i: (0, 0)),
            pl.BlockSpec((1, ATT_QKV_W), lambda i: (0, 0)),
        ],
        out_specs=pl.BlockSpec((tm, ATT_QKV_W), lambda i: (i, 0)),
        out_shape=jax.ShapeDtypeStruct((n, ATT_QKV_W), BF16),
        compiler_params=_cparams("parallel"),
        name="qkv_proj",
    )(x2, g_mix, w_qkv, b_qkv)


HZ_TN = 1024
HZ_COLS = 4 * HGRN_W + 2 * D_MODEL
HZ_F_BLOCK = 1


def _hz_kernel(x_ref, g_ref, w_ref, hz_ref, f_ref, u_scr):
    j = pl.program_id(1)

    @pl.when(j == 0)
    def _():
        u_scr[...] = (_rms(x_ref[...]) * g_ref[...]).astype(BF16)

    acc = jnp.dot(u_scr[...], w_ref[...], preferred_element_type=F32)
    hz_ref[...] = acc.astype(BF16)

    @pl.when(j == HZ_F_BLOCK)
    def _():
        f_ref[...] = acc


def _hz_proj(x2, g_mix, w_hz, tm):
    n = x2.shape[0]
    return pl.pallas_call(
        _hz_kernel,
        grid=(n // tm, HZ_COLS // HZ_TN),
        in_specs=[
            pl.BlockSpec((tm, D_MODEL), lambda i, j: (i, 0)),
            pl.BlockSpec((1, D_MODEL), lambda i, j: (0, 0)),
            pl.BlockSpec((D_MODEL, HZ_TN), lambda i, j: (0, j)),
        ],
        out_specs=[
            pl.BlockSpec((tm, HZ_TN), lambda i, j: (i, j)),
            pl.BlockSpec((tm, HGRN_W), lambda i, j: (i, 0)),
        ],
        out_shape=[
            jax.ShapeDtypeStruct((n, HZ_COLS), BF16),
            jax.ShapeDtypeStruct((n, HGRN_W), F32),
        ],
        scratch_shapes=[pltpu.VMEM((tm, D_MODEL), BF16)],
        compiler_params=_cparams("parallel", "arbitrary"),
        name="hz_proj",
    )(x2, g_mix, w_hz)


def _t5_bucket_table():
    qi = np.arange(ATT_BLOCK)[:, None]
    kj = np.arange(2 * ATT_BLOCK)[None, :]
    dist = qi + ATT_BLOCK - kj
    exact = REL_BUCKETS // 2
    d = np.maximum(dist, 0)
    large = exact + (np.log(np.maximum(d, 1).astype(np.float32) / exact)
                     / math.log(REL_MAX_DIST / exact) * (REL_BUCKETS - exact)).astype(np.int32)
    large = np.minimum(large, REL_BUCKETS - 1)
    return np.where(d < exact, d, large).astype(np.int32)


def _bias_kernel(bucket_ref, rel_ref, o_ref):
    bk = bucket_ref[...]
    qi = lax.broadcasted_iota(I32, bk.shape, 0)
    kj = lax.broadcasted_iota(I32, bk.shape, 1)
    dist = qi + ATT_BLOCK - kj
    in_window = (dist >= 0) & (dist < WINDOW)
    first_block = in_window & (kj >= ATT_BLOCK)
    neg = jnp.float32(-jnp.inf)
    for h in range(ATT_Q_HEADS):
        acc = jnp.zeros(bk.shape, F32)
        for b in range(REL_BUCKETS):
            acc = jnp.where(bk == b, rel_ref[b, h], acc)
        o_ref[0, h] = jnp.where(first_block, acc, neg)
        o_ref[1, h] = jnp.where(in_window, acc, neg)


def _bias_table(rel_bias):
    bucket = jnp.asarray(_t5_bucket_table())
    return pl.pallas_call(
        _bias_kernel,
        in_specs=[
            pl.BlockSpec(memory_space=pltpu.VMEM),
            pl.BlockSpec(memory_space=pltpu.SMEM),
        ],
        out_specs=pl.BlockSpec(memory_space=pltpu.VMEM),
        out_shape=jax.ShapeDtypeStruct((2, ATT_Q_HEADS, ATT_BLOCK, 2 * ATT_BLOCK), F32),
        name="bias_table",
    )(bucket, rel_bias)


SWA_BLOCKS = 2


def _swa_kernel(q_ref, kvc_ref, kvp_ref, bias_ref, sink_ref, o_ref):
    hd = ATT_HEAD_DIM
    blk = ATT_BLOCK
    nt = (((1,), (1,)), ((), ()))
    step = pl.program_id(1)
    left = lax.broadcasted_iota(I32, (2 * blk, 2 * hd), 1) < hd
    left_q = lax.broadcasted_iota(I32, (blk, 2 * hd), 1) < hd
    one = jnp.ones((), F32)
    zero = jnp.zeros((), F32)
    for sb in range(SWA_BLOCKS):
        rows = slice(sb * blk, (sb + 1) * blk)
        prev = kvp_ref[...] if sb == 0 else kvc_ref[(sb - 1) * blk:sb * blk, :]
        band = jnp.concatenate([prev, kvc_ref[rows, :]], axis=0).astype(F32)
        kcol = band[:, :2 * hd] * (hd ** -0.5)
        vcol = band[:, 2 * hd:]
        krot = pltpu.roll(kcol, hd, axis=1)
        vrot = pltpu.roll(vcol, hd, axis=1)
        table = jnp.minimum(step * SWA_BLOCKS + sb, 1) if sb == 0 else 1
        for kk in range(ATT_KV_HEADS):
            k_src, k_alt = (kcol, krot) if kk == 0 else (krot, kcol)
            v_src, v_alt = (vcol, vrot) if kk == 0 else (vrot, vcol)
            k_a = jnp.where(left, k_src, zero).astype(BF16)
            k_b = jnp.where(left, zero, k_alt).astype(BF16)
            v_a = jnp.where(left, v_src, one).astype(BF16)
            v_b = jnp.where(left, one, v_alt).astype(BF16)
            pairs = range(kk * ATT_GROUP // 2, (kk + 1) * ATT_GROUP // 2)
            scores = []
            for j in pairs:
                qp = q_ref[rows, 2 * hd * j:2 * hd * (j + 1)]
                scores.append((lax.dot_general(qp, k_a, nt, preferred_element_type=F32),
                               lax.dot_general(qp, k_b, nt, preferred_element_type=F32)))
            probs, sink_terms = [], []
            for j, pair in zip(pairs, scores):
                p_pair, t_pair = [], []
                for h, s in zip((2 * j, 2 * j + 1), pair):
                    s = s + bias_ref[table, h]
                    sink = sink_ref[0, h]
                    m = jnp.maximum(jnp.max(s, axis=-1, keepdims=True), sink)
                    p_pair.append(jnp.exp(s - m).astype(BF16))
                    t_pair.append(jnp.exp(sink - m))
                probs.append(p_pair)
                sink_terms.append(t_pair)
            ext = [(jnp.dot(pa, v_a, preferred_element_type=F32), jnp.dot(pb, v_b, preferred_element_type=F32))
                   for pa, pb in probs]
            for j, (ea, eb), (ta, tb) in zip(pairs, ext, sink_terms):
                num = jnp.where(left_q, ea, eb)
                den = pltpu.roll(jnp.where(left_q, eb, ea), hd, axis=1)
                den = den + jnp.where(left_q, ta, tb)
                o_ref[rows, 2 * hd * j:2 * hd * (j + 1)] = (num / den).astype(o_ref.dtype)


def _swa(qkv, bias_tab, sinks, batch, seq):
    n = qkv.shape[0]
    nb = seq // ATT_BLOCK
    assert nb % SWA_BLOCKS == 0
    ns = nb // SWA_BLOCKS
    rows = SWA_BLOCKS * ATT_BLOCK
    kv_col = ATT_Q_W // (2 * ATT_KV_W)

    return pl.pallas_call(
        _swa_kernel,
        grid=(batch, ns),
        in_specs=[
            pl.BlockSpec((rows, ATT_Q_W), lambda b, i: (b * ns + i, 0)),
            pl.BlockSpec((rows, 2 * ATT_KV_W), lambda b, i: (b * ns + i, kv_col)),
            pl.BlockSpec((ATT_BLOCK, 2 * ATT_KV_W),
                         lambda b, i: (b * nb + jnp.maximum(i * SWA_BLOCKS - 1, 0), kv_col)),
            pl.BlockSpec((2, ATT_Q_HEADS, ATT_BLOCK, 2 * ATT_BLOCK), lambda b, i: (0, 0, 0, 0)),
            pl.BlockSpec(memory_space=pltpu.SMEM),
        ],
        out_specs=pl.BlockSpec((rows, ATT_Q_W), lambda b, i: (b * ns + i, 0)),
        out_shape=jax.ShapeDtypeStruct((n, ATT_Q_W), BF16),
        compiler_params=_cparams("parallel", "arbitrary"),
        name="swa",
    )(qkv, qkv, qkv, bias_tab, sinks)


def _cumsum_rows(tri_bf16, g):
    w = g.shape[1]
    g1 = g.astype(BF16)
    r1 = g - g1.astype(F32)
    g2 = r1.astype(BF16)
    g3 = (r1 - g2.astype(F32)).astype(BF16)
    parts = jnp.dot(tri_bf16, jnp.concatenate([g1, g2, g3], axis=1), preferred_element_type=F32)
    return parts[:, :w] + (parts[:, w:2 * w] + parts[:, 2 * w:])


def _hgrn_kernel(q_ref, f_ref, i_ref, g_ref, lbl_ref, gh_ref, o_ref, st_ref, b_scr, k_scr, *, chunks):
    C = HGRN_CHUNK
    ref_row = C // 2 - 1

    @pl.when(pl.program_id(1) == 0)
    def _():
        st_ref[...] = jnp.zeros_like(st_ref)

    l = lbl_ref[...]
    e = jnp.exp(l - jnp.max(l, axis=0, keepdims=True))
    lb_all = e[0:1, :] / jnp.sum(e, axis=0, keepdims=True)
    f_mid = 0.5 * (1.0 + lb_all)
    f_half = 0.5 * (1.0 - lb_all)

    row = lax.broadcasted_iota(I32, (C, C), 0)
    col = lax.broadcasted_iota(I32, (C, C), 1)
    causal = row >= col
    tri = causal.astype(F32).astype(BF16)
    gain = gh_ref[...]

    def chunk_rows(c):
        return pl.ds(pl.multiple_of(c * C, C), C)

    def decays(c, slot):
        swing = f_half * jnp.tanh(0.5 * f_ref[chunk_rows(c), :])
        k_scr[slot] = f_half - swing
        b_scr[slot] = _cumsum_rows(tri, jnp.log(f_mid + swing)) * LOG2_E

    def outputs(c, slot):
        rows = chunk_rows(c)
        b = b_scr[slot]
        k = k_scr[slot]
        qs = _silu(q_ref[rows, :].astype(F32))
        b_ref = b[ref_row:ref_row + 1, :]
        b_last = b[C - 1:C, :]
        qe_all = (qs * jnp.exp2(b)).astype(BF16)
        qa_all = (qs * jnp.exp2(b - b_ref)).astype(BF16)
        ka_all = (k * jnp.exp2(b_ref - b)).astype(BF16)
        kl_all = (k * jnp.exp2(b_last - b)).astype(BF16)
        decay = jnp.exp2(b_last)
        gate = _silu(g_ref[rows, :].astype(F32))
        nt = (((1,), (1,)), ((), ()))
        heads = [slice(HGRN_DK * h, HGRN_DK * (h + 1)) for h in range(HGRN_HEADS)]
        vs = [i_ref[rows, cs] for cs in heads]
        sts = [st_ref[h] for h in range(HGRN_HEADS)]
        attn = [lax.dot_general(qa_all[:, cs], ka_all[:, cs], nt, preferred_element_type=F32) for cs in heads]
        inter = [lax.dot_general(qe_all[:, cs], st.astype(BF16), nt, preferred_element_type=F32)
                 for cs, st in zip(heads, sts)]
        upd = [lax.dot_general(v, kl_all[:, cs], (((0,), (0,)), ((), ())), preferred_element_type=F32)
               for cs, v in zip(heads, vs)]
        for h, cs in enumerate(heads):
            st_ref[h] = sts[h] * decay[:, cs] + upd[h]
        intra = [jnp.dot(jnp.where(causal, a, 0.0).astype(BF16), v, preferred_element_type=F32)
                 for a, v in zip(attn, vs)]
        for h, cs in enumerate(heads):
            o = inter[h] + intra[h]
            o_ref[rows, cs] = (_rms(o) * gain * gate[:, cs]).astype(o_ref.dtype)

    decays(0, 0)

    def pair(p, carry):
        c = 2 * p
        decays(c + 1, 1)
        outputs(c, 0)
        decays(jnp.minimum(c + 2, chunks - 1), 0)
        outputs(c + 1, 1)
        return carry

    lax.fori_loop(0, chunks // 2, pair, 0)


def _hgrn(hz, f_raw, lb_logits, g_hgrn, batch, seq, tc):
    n = hz.shape[0]
    nt = seq // tc
    blk = lambda col: pl.BlockSpec((tc, HGRN_W), lambda b, t, col=col: (b * nt + t, col))
    return pl.pallas_call(
        functools.partial(_hgrn_kernel, chunks=tc // HGRN_CHUNK),
        grid=(batch, nt),
        in_specs=[
            blk(0),
            pl.BlockSpec((tc, HGRN_W), lambda b, t: (b * nt + t, 0)),
            blk(2),
            blk(3),
            pl.BlockSpec((2, HGRN_W), lambda b, t: (0, 0)),
            pl.BlockSpec((1, HGRN_DV), lambda b, t: (0, 0)),
        ],
        out_specs=pl.BlockSpec((tc, HGRN_W), lambda b, t: (b * nt + t, 0)),
        out_shape=jax.ShapeDtypeStruct((n, HGRN_W), BF16),
        scratch_shapes=[pltpu.VMEM((HGRN_HEADS, HGRN_DV, HGRN_DK), F32),
                        pltpu.VMEM((2, HGRN_CHUNK, HGRN_W), F32), pltpu.VMEM((2, HGRN_CHUNK, HGRN_W), F32)],
        compiler_params=_cparams("parallel", "arbitrary"),
        name="hgrn",
    )(hz, f_raw, hz, hz, lb_logits, g_hgrn)


def _split_hi_lo(a):
    hi = a.astype(BF16)
    lo = (a - hi.astype(F32)).astype(BF16)
    return hi, lo


def _mix_kernel(oa_ref, or_ref, ga_ref, gr_ref, x_ref, wa_ref, wr_ref, wo_ref, gf_ref, wrt_ref, brt_ref,
                h_ref, xn_ref, lg_ref):
    dot = functools.partial(jnp.dot, preferred_element_type=F32)
    w_hi, w_lo = _split_hi_lo(wrt_ref[...])
    tm = x_ref.shape[0]
    sub = tm // MIX_SUBTILES
    for r in range(MIX_SUBTILES):
        rows = slice(r * sub, (r + 1) * sub)
        ya = dot(oa_ref[rows, :], wa_ref[...])
        yr = dot(or_ref[rows, :], wr_ref[...])
        mix = _sigmoid(ga_ref[rows, :].astype(F32)) * ya + _sigmoid(gr_ref[rows, :].astype(F32)) * yr
        h = x_ref[rows, :] + dot(mix.astype(BF16), wo_ref[...])
        h_ref[rows, :] = h
        xn = _rms(h) * gf_ref[...]
        _store_token_rows(xn_ref, _pack_bf16_pair(xn[:, :HALF], xn[:, HALF:]), first=r * sub)
        x_hi, x_lo = _split_hi_lo(xn)
        lg_ref[rows, :] = dot(x_hi, w_hi) + (dot(x_hi, w_lo) + dot(x_lo, w_hi)) + brt_ref[...]


def _mix_out(o_att, o_rec, hz, x2, w_up_att, w_up_rec, w_out, g_ffn, w_router, b_router, tm):
    n = x2.shape[0]
    const = lambda shape: pl.BlockSpec(shape, lambda i: (0,) * len(shape))
    ga_col = 4 * HGRN_W // D_MODEL
    return pl.pallas_call(
        _mix_kernel,
        grid=(n // tm,),
        in_specs=[
            pl.BlockSpec((tm, ATT_Q_W), lambda i: (i, 0)),
            pl.BlockSpec((tm, HGRN_W), lambda i: (i, 0)),
            pl.BlockSpec((tm, D_MODEL), lambda i: (i, ga_col)),
            pl.BlockSpec((tm, D_MODEL), lambda i: (i, ga_col + 1)),
            pl.BlockSpec((tm, D_MODEL), lambda i: (i, 0)),
            const((ATT_Q_W, D_MODEL)),
            const((HGRN_W, D_MODEL)),
            const((D_MODEL, D_MODEL)),
            const((1, D_MODEL)),
            const((D_MODEL, ROUTER_W)),
            const((1, ROUTER_W)),
        ],
        out_specs=[
            pl.BlockSpec((tm, D_MODEL), lambda i: (i, 0)),
            pl.BlockSpec((tm * ROW_TILE, LANES), lambda i: (i, 0)),
            pl.BlockSpec((tm, ROUTER_W), lambda i: (i, 0)),
        ],
        out_shape=[
            jax.ShapeDtypeStruct((n, D_MODEL), F32),
            jax.ShapeDtypeStruct((n * ROW_TILE, LANES), U32),
            jax.ShapeDtypeStruct((n, ROUTER_W), F32),
        ],
        compiler_params=_cparams("parallel"),
        name="mix_out",
    )(o_att, o_rec, hz, hz, x2, w_up_att, w_up_rec, w_out, g_ffn, w_router, b_router)


def _route_choice(lgt):
    row = lax.broadcasted_iota(I32, lgt.shape, 0).astype(F32)
    neg = jnp.float32(-jnp.inf)
    far = float(ROUTER_W)
    lc = jnp.where(row < MOE_GROUPS, lgt, neg)
    cmax = jnp.max(lc, axis=0, keepdims=True)
    p_sel = 1.0 / jnp.sum(jnp.exp(lc - cmax), axis=0, keepdims=True)
    grp = jnp.min(jnp.where(lc == cmax, row, far), axis=0, keepdims=True)
    fine = row - MOE_GROUPS
    first = grp * MOE_EXPERTS_PER_GROUP
    lf = jnp.where((fine >= first) & (fine < first + MOE_EXPERTS_PER_GROUP), lgt, neg)
    v0 = jnp.max(lf, axis=0, keepdims=True)
    sel0 = fine == jnp.min(jnp.where(lf == v0, fine, far), axis=0, keepdims=True)
    lf1 = jnp.where(sel0, neg, lf)
    v1 = jnp.max(lf1, axis=0, keepdims=True)
    sel1 = fine == jnp.min(jnp.where(lf1 == v1, fine, far), axis=0, keepdims=True)
    t1 = jnp.exp(v1 - v0)
    w0 = p_sel / (1.0 + t1)
    w1 = p_sel * t1 / (1.0 + t1)
    return sel0, sel1, w0, w1


def _route_kernel(lg_ref, dest_ref, wts_ref, blk_ref, cnt_ref, *, tm, nblk_pad):
    phase = pl.program_id(0)
    i = pl.program_id(1)
    R = ROUTER_W

    @pl.when((phase == 0) & (i == 0))
    def _():
        cnt_ref[...] = jnp.zeros_like(cnt_ref)

    sel0, sel1, w0, w1 = _route_choice(lg_ref[...].T)
    picks = (sel0 | sel1).astype(F32)
    picked = jnp.sum(picks, axis=1, keepdims=True)

    @pl.when(phase == 0)
    def _():
        cnt_ref[0] = cnt_ref[0] + picked

    @pl.when(phase == 1)
    def _():
        dot = functools.partial(jnp.dot, preferred_element_type=F32)
        nb = jnp.floor((cnt_ref[0] + (EXPERT_ROWS - 1)) * (1.0 / EXPERT_ROWS))
        nb_hi = jnp.floor(nb * (1.0 / 32.0))
        nb_lo = nb - 32.0 * nb_hi
        r = lax.broadcasted_iota(I32, (R, R), 0)
        c = lax.broadcasted_iota(I32, (R, R), 1)
        lower = (c < r).astype(F32).astype(BF16)
        start_blk = 32.0 * dot(lower, nb_hi.astype(BF16)) + dot(lower, nb_lo.astype(BF16))

        @pl.when(i == 0)
        def _():
            cnt_ref[1] = jnp.zeros((R, LANES), F32)
            end_col = (start_blk + nb)[:, 0:1]
            erow = lax.broadcasted_iota(I32, (R, nblk_pad), 0)
            is_expert = (erow >= MOE_GROUPS) & (erow < MOE_GROUPS + MOE_EXPERTS)
            bidx = lax.broadcasted_iota(I32, (R, nblk_pad), 1).astype(F32)
            be = jnp.sum((is_expert & (end_col <= bidx)).astype(F32), axis=0, keepdims=True)
            be = jnp.minimum(be, float(MOE_EXPERTS - 1))
            n_used = jnp.sum(nb[:, 0:1], axis=0, keepdims=True)
            out_row = lax.broadcasted_iota(I32, blk_ref.shape, 0)
            blk_ref[...] = jnp.where(out_row == 0, be, n_used).astype(I32)

        tr = lax.broadcasted_iota(I32, (tm, tm), 0)
        tc = lax.broadcasted_iota(I32, (tm, tm), 1)
        earlier = (tr < tc).astype(F32).astype(BF16)
        before = dot(picks.astype(BF16), earlier)
        pos = (start_blk * float(EXPERT_ROWS) + cnt_ref[1])[:, 0:1] + before
        dest_ref[0:1, :] = jnp.sum(jnp.where(sel0, pos, 0.0), axis=0, keepdims=True).astype(I32)
        dest_ref[1:2, :] = jnp.sum(jnp.where(sel1, pos, 0.0), axis=0, keepdims=True).astype(I32)
        cnt_ref[1] = cnt_ref[1] + picked
        wrow = lax.broadcasted_iota(I32, (R, tm), 0)
        wts_ref[...] = jnp.where(wrow == 0, w0, jnp.where(wrow == 1, w1, 0.0)).T


def _route(logits, tm, nblk_pad):
    n = logits.shape[0]
    return pl.pallas_call(
        functools.partial(_route_kernel, tm=tm, nblk_pad=nblk_pad),
        grid=(2, n // tm),
        in_specs=[pl.BlockSpec((tm, ROUTER_W), lambda p, i: (i, 0))],
        out_specs=[
            pl.BlockSpec((2, tm), lambda p, i: (0, i * p)),
            pl.BlockSpec((tm, LANES), lambda p, i: (i * p, 0)),
            pl.BlockSpec((8, nblk_pad), lambda p, i: (0, 0)),
        ],
        out_shape=[
            jax.ShapeDtypeStruct((2, n), I32),
            jax.ShapeDtypeStruct((n, LANES), F32),
            jax.ShapeDtypeStruct((8, nblk_pad), I32),
        ],
        scratch_shapes=[pltpu.VMEM((2, ROUTER_W, LANES), F32)],
        compiler_params=_cparams("arbitrary", "arbitrary"),
        name="route",
    )(logits)


ISSUE_UNROLL = 8
COMBINE_SUBTILES = 2
COMBINE_LEAD = 2
MIX_SUBTILES = 1


def _dispatch_kernel(dest_ref, blk_ref, nused_ref, xn_ref, xs_ref, zbuf, sem, zsem, *, tm, nblk):
    i = pl.program_id(0)
    base = i * tm
    block_rows = EXPERT_ROWS * ROW_TILE

    @pl.when(i == 0)
    def _():
        zbuf[...] = jnp.zeros_like(zbuf)
        last_used = nused_ref[0] - 1

        def needs_clear(j):
            return (j >= last_used) | (blk_ref[j] != blk_ref[jnp.minimum(j + 1, nblk - 1)])

        def clear(j):
            rows = pl.ds(pl.multiple_of(j * block_rows, block_rows), block_rows)
            return pltpu.make_async_copy(zbuf, xs_ref.at[rows, :], zsem)

        def start(j, c):
            @pl.when(needs_clear(j))
            def _():
                clear(j).start()
            return c

        def wait(j, c):
            @pl.when(needs_clear(j))
            def _():
                clear(j).wait()
            return c

        lax.fori_loop(0, nblk, start, 0)
        lax.fori_loop(0, nblk, wait, 0)

    n_tokens = tm * pl.num_programs(0)

    def issue(t, c):
        for k in range(2):
            pltpu.make_async_copy(_token_row(xn_ref, t),
                                  _token_row(xs_ref, dest_ref[k * n_tokens + base + t]), sem).start(priority=k)
        return c

    lax.fori_loop(0, tm, issue, 0, unroll=ISSUE_UNROLL)
    for k in range(2):
        pltpu.make_async_copy(xn_ref, xs_ref.at[pl.ds(0, tm * ROW_TILE), :], sem).wait()


def _dispatch(dest_flat, blk_e, n_used, xn_packed, cap, tm):
    n = xn_packed.shape[0] // ROW_TILE
    nblk = cap // EXPERT_ROWS
    return pl.pallas_call(
        functools.partial(_dispatch_kernel, tm=tm, nblk=nblk),
        grid_spec=pltpu.PrefetchScalarGridSpec(
            num_scalar_prefetch=3,
            grid=(n // tm,),
            in_specs=[pl.BlockSpec((tm * ROW_TILE, LANES), lambda i, d, b, u: (i, 0))],
            out_specs=pl.BlockSpec(memory_space=pl.ANY),
            scratch_shapes=[pltpu.VMEM((EXPERT_ROWS * ROW_TILE, LANES), U32),
                            pltpu.SemaphoreType.DMA, pltpu.SemaphoreType.DMA],
        ),
        out_shape=jax.ShapeDtypeStruct((cap * ROW_TILE, LANES), U32),
        compiler_params=_cparams("arbitrary"),
        name="dispatch",
    )(dest_flat, blk_e, n_used, xn_packed)


def _expert_kernel(blk_ref, nused_ref, xs_ref, wg_hbm, wu_hbm, wd_hbm, y_ref,
                   wg_f, wu_f, wd_f, wg_b, wu_b, wd_b, slot_ref, sems, *, nblk):
    i = pl.program_id(0)
    nused = nused_ref[0]
    active = i < nused
    expert = blk_ref[i]
    new_expert = (i == 0) | (expert != blk_ref[jnp.maximum(i - 1, 0)])

    def fetch(e, slot):
        return [pltpu.make_async_copy(w_hbm.at[e], w_f.at[slot], sems.at[slot])
                for w_hbm, w_f in ((wg_hbm, wg_f), (wu_hbm, wu_f), (wd_hbm, wd_f))]

    @pl.when(jnp.logical_not(active))
    def _():
        y_ref[...] = jnp.zeros_like(y_ref)

    @pl.when(i == 0)
    def _():
        slot_ref[0] = 0
        for c in fetch(expert, 0):
            c.start()

    @pl.when(active & new_expert)
    def _():
        slot = slot_ref[0]
        for c in fetch(expert, slot):
            c.wait()
        wg_b[...] = wg_f[slot].astype(BF16)
        wu_b[...] = wu_f[slot].astype(BF16)
        wd_b[...] = wd_f[slot].astype(BF16)
        nxt = lax.while_loop(lambda j: (j < nused) & (blk_ref[jnp.minimum(j, nblk - 1)] == expert),
                             lambda j: j + 1, i + 1)

        @pl.when(nxt < nused)
        def _():
            for c in fetch(blk_ref[jnp.minimum(nxt, nblk - 1)], 1 - slot):
                c.start()

        slot_ref[0] = 1 - slot

    @pl.when(active)
    def _():
        dot = functools.partial(jnp.dot, preferred_element_type=F32)
        lo, hi = _unpack_bf16_pair(_load_token_rows(xs_ref, EXPERT_ROWS))
        lo = lo.astype(BF16)
        hi = hi.astype(BF16)
        g = dot(lo, wg_b[:HALF, :]) + dot(hi, wg_b[HALF:, :])
        u = dot(lo, wu_b[:HALF, :]) + dot(hi, wu_b[HALF:, :])
        hid = (_silu(g) * u).astype(BF16)
        y = dot(hid, wd_b[...])
        _store_token_rows(y_ref, _pack_bf16_pair(y[:, :HALF], y[:, HALF:]))


def _experts(blk_e, n_used, xs, w_gate, w_up, w_down):
    cap = xs.shape[0] // ROW_TILE
    nblk = cap // EXPERT_ROWS
    row = lambda i, blk, nu: (jnp.minimum(i, nu[0] - 1), 0)
    hbm = pl.BlockSpec(memory_space=pl.ANY)
    up_shape, down_shape = (D_MODEL, MOE_HIDDEN), (MOE_HIDDEN, D_MODEL)
    return pl.pallas_call(
        functools.partial(_expert_kernel, nblk=nblk),
        grid_spec=pltpu.PrefetchScalarGridSpec(
            num_scalar_prefetch=2,
            grid=(nblk,),
            in_specs=[pl.BlockSpec((EXPERT_ROWS * ROW_TILE, LANES), row), hbm, hbm, hbm],
            out_specs=pl.BlockSpec((EXPERT_ROWS * ROW_TILE, LANES), lambda i, blk, nu: (i, 0)),
            scratch_shapes=[
                pltpu.VMEM((2,) + up_shape, F32), pltpu.VMEM((2,) + up_shape, F32),
                pltpu.VMEM((2,) + down_shape, F32),
                pltpu.VMEM(up_shape, BF16), pltpu.VMEM(up_shape, BF16), pltpu.VMEM(down_shape, BF16),
                pltpu.SMEM((1,), I32), pltpu.SemaphoreType.DMA((2,)),
            ],
        ),
        out_shape=jax.ShapeDtypeStruct((cap * ROW_TILE, LANES), U32),
        compiler_params=_cparams("arbitrary"),
        name="experts",
    )(blk_e, n_used, xs, w_gate, w_up, w_down)


def _combine_kernel(dest_ref, h_ref, wts_ref, p_ref, y_ref, wpi_ref, gp_ref, wpg_ref, gfin_ref, o_ref,
                    *scratch, tm):
    *bufs, sems = scratch
    i = pl.program_id(0)
    last = pl.num_programs(0) - 1
    n_tokens = tm * pl.num_programs(0)
    dot = functools.partial(jnp.dot, preferred_element_type=F32)

    def row_copy(tile, t, k, buf, sem):
        return pltpu.make_async_copy(_token_row(y_ref, dest_ref[k * n_tokens + tile * tm + t]),
                                     _token_row(buf.at[k], t), sem)

    def wait_tile(buf, sem):
        for k in range(2):
            pltpu.make_async_copy(y_ref.at[pl.ds(0, tm * ROW_TILE), :], buf.at[k], sem).wait()

    @pl.when(i == 0)
    def _():
        def issue(t, c):
            for tile in range(COMBINE_LEAD):
                for k in range(2):
                    row_copy(jnp.minimum(tile, last), t, k, bufs[tile], sems.at[tile]).start(priority=1)
            return c

        lax.fori_loop(0, tm, issue, 0, unroll=ISSUE_UNROLL)

    def step(r):
        cur, cur_sem = bufs[r], sems.at[r]
        ahead = (r + COMBINE_LEAD) % len(bufs)
        nxt, nxt_sem = bufs[ahead], sems.at[ahead]
        wait_tile(cur, cur_sem)
        nxt_tile = jnp.minimum(i + COMBINE_LEAD, last)
        for t in range(tm):
            for k in range(2):
                row_copy(nxt_tile, t, k, nxt, nxt_sem).start(priority=1)

        sub = tm // COMBINE_SUBTILES
        for r in range(COMBINE_SUBTILES):
            rows = slice(r * sub, (r + 1) * sub)
            ple = _rms(dot(p_ref[rows, :].astype(BF16), wpi_ref[...])) * gp_ref[...]
            w = wts_ref[rows, :]
            w0 = w[:, 0:1]
            w1 = w[:, 1:2]
            lo0, hi0 = _unpack_bf16_pair(_load_token_rows(cur.at[0], sub, first=r * sub))
            lo1, hi1 = _unpack_bf16_pair(_load_token_rows(cur.at[1], sub, first=r * sub))
            moe = jnp.concatenate([w0 * lo0 + w1 * lo1, w0 * hi0 + w1 * hi1], axis=-1)
            h = h_ref[rows, :] + moe
            gate = _sigmoid(dot(_rms(h).astype(BF16), wpg_ref[...]))
            h = h + ple * gate
            o_ref[rows, :] = _rms(h) * gfin_ref[...]

        @pl.when(i == last)
        def _():
            for other in range(len(bufs)):
                if other != r:
                    wait_tile(bufs[other], sems.at[other])

    phase = lax.rem(i, len(bufs))
    for r in range(len(bufs)):
        pl.when(phase == r)(functools.partial(step, r))


def _combine_ple(dest_flat, h1, wts, p2, y, w_ple_in, g_ple, w_ple_gate, g_final, tm):
    n = h1.shape[0]
    const = lambda shape: pl.BlockSpec(shape, lambda i, d: (0,) * len(shape))
    return pl.pallas_call(
        functools.partial(_combine_kernel, tm=tm),
        grid_spec=pltpu.PrefetchScalarGridSpec(
            num_scalar_prefetch=1,
            grid=(n // tm,),
            in_specs=[
                pl.BlockSpec((tm, D_MODEL), lambda i, d: (i, 0)),
                pl.BlockSpec((tm, LANES), lambda i, d: (i, 0)),
                pl.BlockSpec((tm, PLE_DIM), lambda i, d: (i, 0)),
                pl.BlockSpec(memory_space=pl.ANY),
                const((PLE_DIM, D_MODEL)),
                const((1, D_MODEL)),
                const((D_MODEL, D_MODEL)),
                const((1, D_MODEL)),
            ],
            out_specs=pl.BlockSpec((tm, D_MODEL), lambda i, d: (i, 0)),
            scratch_shapes=[pltpu.VMEM((2, tm * ROW_TILE, LANES), U32)] * (COMBINE_LEAD + 1)
            + [pltpu.SemaphoreType.DMA((COMBINE_LEAD + 1,))],
        ),
        out_shape=jax.ShapeDtypeStruct((n, D_MODEL), F32),
        compiler_params=_cparams("arbitrary"),
        name="combine_ple",
    )(dest_flat, h1, wts, p2, y, w_ple_in, g_ple, w_ple_gate, g_final)


def _tile(n, pref):
    t = min(n, pref)
    assert n % t == 0, (n, t)
    return t


def kernel(x, p, g_mix, w_in, b_qkv, sinks, rel_bias, lb_logits, g_hgrn, w_up_att, w_up_rec, w_out, g_ffn,
           w_coarse, b_coarse, w_fine, b_fine, w_gate, w_up, w_down, w_ple_in, g_ple, w_ple_gate, g_final):
    B, T, D = x.shape
    assert D == D_MODEL and T % ATT_BLOCK == 0 and lb_logits.shape[0] == 2
    n = B * T
    layer = 0
    x2 = x.reshape(n, D)
    row = lambda a: a.reshape(1, -1)

    w_qkv = w_in[layer][:, :ATT_QKV_W].astype(BF16)
    w_hz = w_in[layer][:, ATT_QKV_W:].astype(BF16)
    qkv = _qkv_proj(x2, row(g_mix[layer]), w_qkv, row(b_qkv[layer]), _tile(n, 1024))
    hz, f_raw = _hz_proj(x2, row(g_mix[layer]), w_hz, _tile(n, 1024))

    o_att = _swa(qkv, _bias_table(rel_bias), row(sinks[layer]), B, T)
    o_rec = _hgrn(hz, f_raw, lb_logits, row(g_hgrn[layer]), B, T, _tile(T, 1024))

    w_router = jnp.concatenate(
        [w_coarse[layer], w_fine[layer], jnp.zeros((D, ROUTER_W - MOE_GROUPS - MOE_EXPERTS), F32)], axis=1)
    b_router = jnp.concatenate(
        [b_coarse[layer], b_fine[layer], jnp.zeros((ROUTER_W - MOE_GROUPS - MOE_EXPERTS,), F32)]).reshape(1, -1)
    h1, xn_packed, logits = _mix_out(
        o_att, o_rec, hz, x2, w_up_att[layer].astype(BF16), w_up_rec[layer].astype(BF16),
        w_out[layer].astype(BF16), row(g_ffn[layer]), w_router, b_router, _tile(n, 256))

    cap = 2 * n + MOE_EXPERTS * EXPERT_ROWS
    nblk = cap // EXPERT_ROWS
    nblk_pad = (nblk + LANES - 1) // LANES * LANES
    dest, wts, blk_tab = _route(logits, _tile(n, 512), nblk_pad)
    dest_flat = dest.reshape(2 * n)
    blk_e = blk_tab[0, :nblk]
    n_used = blk_tab[1, 0:1]

    xs = _dispatch(dest_flat, blk_e, n_used, xn_packed, cap, _tile(n, 1024))
    y = _experts(blk_e, n_used, xs, w_gate[layer], w_up[layer], w_down[layer])
    out = _combine_ple(dest_flat, h1, wts, p[layer].reshape(n, PLE_DIM), y, w_ple_in[layer].astype(BF16),
                       row(g_ple[layer]), w_ple_gate[layer].astype(BF16), row(g_final), _tile(n, 256))
    return out.reshape(B, T, D)
```
